```python
import math
import jax, jax.numpy as jnp
from jax import lax
import numpy as np

D_MODEL = 2048
BATCH = 4
SEQ = 4096
DEPTH = 2
DEC_BATCH = 128
DEC_SEQ = 1
PAST_LEN = 16384
PAGE_SIZE = 128

MIX_WIDTH = D_MODEL
M_HEADS = 4
M_DV = MIX_WIDTH // 2 // M_HEADS
M_DK = M_DV // 2
M_CHUNK = 128
A_HEADS = 8
A_DV = MIX_WIDTH // 2 // A_HEADS
A_NOPE = 128
A_ROPE = 64
Q_RANK = 384
KV_RANK = 256
ROPE_THETA = 10000.0
Q_BLOCK = 128
D_FF = 4 * D_MODEL
EPS = 1e-6
M_QK_W = M_HEADS * M_DK
M_V_W = M_HEADS * M_DV
IN_DIM = 2 * M_QK_W + 2 * M_V_W + 2 * M_HEADS + Q_RANK + KV_RANK + A_ROPE

kernel_name = "hymba_mlstm_mla_decoder_step"


def rmsnorm(x, g):
    xf = x.astype(jnp.float32)
    r = lax.rsqrt(jnp.mean(xf * xf, axis=-1, keepdims=True) + EPS)
    return (xf * r).astype(x.dtype) * g


def split_cols(u):
    sizes = (M_QK_W, M_QK_W, M_V_W, M_V_W, M_HEADS, M_HEADS, Q_RANK, KV_RANK, A_ROPE)
    idx = np.cumsum(sizes)[:-1].tolist()
    return jnp.split(u, idx, axis=-1)


def rope_tables(pos):
    freqs = ROPE_THETA ** (-jnp.arange(0, A_ROPE, 2, dtype=jnp.float32) / A_ROPE)
    ang = pos[:, None] * freqs[None, :]
    return jnp.cos(ang), jnp.sin(ang)


def apply_rope(x, cos, sin):
    cos = cos.astype(x.dtype)
    sin = sin.astype(x.dtype)
    x1, x2 = jnp.split(x, 2, axis=-1)
    return jnp.concatenate([x1 * cos - x2 * sin, x2 * cos + x1 * sin], axis=-1)


def mlstm_scan(q, k, v, ig, lf, C0, n0, m0):
    B, H, L, _ = q.shape
    cs = M_CHUNK if L % M_CHUNK == 0 else L
    nc = L // cs
    def chunks(a):
        return jnp.moveaxis(a.reshape(a.shape[:2] + (nc, cs) + a.shape[3:]), 2, 0)
    causal = jnp.tril(jnp.ones((cs, cs), dtype=bool))

    def step(carry, inp):
        C, n, m = carry
        qc, kc, vc, ic, fc = inp
        b = jnp.cumsum(fc, axis=-1)
        log_inter = b + m[..., None]
        Dm = b[..., :, None] - b[..., None, :] + ic[..., None, :]
        Dm = jnp.where(causal, Dm, -jnp.inf)
        m_t = jnp.maximum(log_inter, jnp.max(Dm, axis=-1))
        w_inter = jnp.exp(log_inter - m_t)
        S = jnp.einsum('bhtd,bhsd->bhts', qc, kc) * jnp.exp(Dm - m_t[..., None])
        num = w_inter[..., None] * jnp.einsum('bhvd,bhtd->bhtv', C, qc) + jnp.einsum('bhts,bhsv->bhtv', S, vc)
        den = w_inter * jnp.einsum('bhd,bhtd->bht', n, qc) + jnp.sum(S, axis=-1)
        h = num / jnp.maximum(jnp.abs(den), jnp.exp(-m_t))[..., None]
        m_new = m_t[..., -1]
        a_inter = jnp.exp(b[..., -1] + m - m_new)
        a_s = jnp.exp(Dm[..., -1, :] - m_new[..., None])
        C_new = a_inter[..., None, None] * C + jnp.einsum('bhsv,bhsd->bhvd', vc * a_s[..., None], kc)
        n_new = a_inter[..., None] * n + jnp.einsum('bhs,bhsd->bhd', a_s, kc)
        return (C_new, n_new, m_new), h

    (C, n, m), hs = lax.scan(step, (C0, n0, m0), (chunks(q), chunks(k), chunks(v), chunks(ig), chunks(lf)))
    h = jnp.moveaxis(hs, 0, 2).reshape(B, H, L, v.shape[-1])
    return h, C, n, m


def mlstm_mixer(q, k, v, o_pre, i_pre, f_pre, b_i, b_f, g_mh, C0, n0, m0):
    B, L, _ = q.shape
    f32 = jnp.float32
    qh = q.reshape(B, L, M_HEADS, M_DK).transpose(0, 2, 1, 3).astype(f32) * (M_DK ** -0.5)
    kh = k.reshape(B, L, M_HEADS, M_DK).transpose(0, 2, 1, 3).astype(f32)
    vh = v.reshape(B, L, M_HEADS, M_DV).transpose(0, 2, 1, 3).astype(f32)
    ig = (i_pre + b_i).astype(f32).transpose(0, 2, 1)
    lf = jax.nn.log_sigmoid((f_pre + b_f).astype(f32)).transpose(0, 2, 1)
    h, C, n, m = mlstm_scan(qh, kh, vh, ig, lf, C0.astype(f32), n0.astype(f32), m0.astype(f32))
    h = h.transpose(0, 2, 1, 3)
    h = h * lax.rsqrt(jnp.mean(h * h, axis=-1, keepdims=True) + EPS)
    h = (h.reshape(B, L, M_V_W) * g_mh.astype(f32)).astype(q.dtype)
    return h * jax.nn.sigmoid(o_pre), C, n, m


def mla_project(q_lat, kv_lat, k_r, g_q, w_uq, g_kv, cos, sin):
    B, L, _ = q_lat.shape
    q = (rmsnorm(q_lat, g_q) @ w_uq).reshape(B, L, A_HEADS, A_NOPE + A_ROPE)
    q_nope = q[..., :A_NOPE]
    q_pe = apply_rope(q[..., A_NOPE:], cos[None, :, None, :], sin[None, :, None, :])
    ckv = rmsnorm(kv_lat, g_kv)
    kpe = apply_rope(k_r, cos[None], sin[None])
    return q_nope, q_pe, ckv, kpe


def mla_prompt(q_lat, kv_lat, k_r, g_q, w_uq, g_kv, w_uk, w_uv, cos, sin):
    B, L, _ = q_lat.shape
    q_nope, q_pe, ckv, kpe = mla_project(q_lat, kv_lat, k_r, g_q, w_uq, g_kv, cos, sin)
    k_nope = (ckv @ w_uk).reshape(B, L, A_HEADS, A_NOPE)
    v = (ckv @ w_uv).reshape(B, L, A_HEADS, A_DV)
    scale = (A_NOPE + A_ROPE) ** -0.5
    nb = L // Q_BLOCK
    qn_b = q_nope.reshape(B, nb, Q_BLOCK, A_HEADS, A_NOPE).transpose(1, 0, 2, 3, 4)
    qp_b = q_pe.reshape(B, nb, Q_BLOCK, A_HEADS, A_ROPE).transpose(1, 0, 2, 3, 4)
    kpos = jnp.arange(L)

    def block(args):
        j, qn, qp = args
        s = (jnp.einsum('bqhd,bkhd->bhqk', qn, k_nope) + jnp.einsum('bqhd,bkd->bhqk', qp, kpe)).astype(jnp.float32) * scale
        qpos = j * Q_BLOCK + jnp.arange(Q_BLOCK)
        s = jnp.where(kpos[None, :] <= qpos[:, None], s, -jnp.inf)
        p = jax.nn.softmax(s, axis=-1).astype(v.dtype)
        return jnp.einsum('bhqk,bkhv->bqhv', p, v)

    o = lax.map(block, (jnp.arange(nb), qn_b, qp_b))
    o = o.transpose(1, 0, 2, 3, 4).reshape(B, L, A_HEADS * A_DV)
    return o, ckv, kpe


def mla_sample(q_lat, kv_lat, k_r, g_q, w_uq, g_kv, w_uk, w_uv, cos, sin, past_ckv, past_kpe):
    B, L, _ = q_lat.shape
    q_nope, q_pe, ckv, kpe = mla_project(q_lat, kv_lat, k_r, g_q, w_uq, g_kv, cos, sin)
    w_uk_h = w_uk.reshape(KV_RANK, A_HEADS, A_NOPE)
    w_uv_h = w_uv.reshape(KV_RANK, A_HEADS, A_DV)
    scale = (A_NOPE + A_ROPE) ** -0.5
    q_abs = jnp.einsum('bqhn,rhn->bqhr', q_nope, w_uk_h)
    s_past = jnp.einsum('bqhr,bkr->bhqk', q_abs, past_ckv) + jnp.einsum('bqhd,bkd->bhqk', q_pe, past_kpe)
    s_new = jnp.einsum('bqhr,bkr->bhqk', q_abs, ckv) + jnp.einsum('bqhd,bkd->bhqk', q_pe, kpe)
    s_new = jnp.where(jnp.tril(jnp.ones((L, L), dtype=bool)), s_new.astype(jnp.float32), -jnp.inf)
    s = jnp.concatenate([s_past.astype(jnp.float32), s_new], axis=-1) * scale
    p = jax.nn.softmax(s, axis=-1).astype(ckv.dtype)
    P = past_ckv.shape[1]
    o_lat = jnp.einsum('bhqk,bkr->bqhr', p[..., :P], past_ckv) + jnp.einsum('bhqk,bkr->bqhr', p[..., P:], ckv)
    o = jnp.einsum('bqhr,rhv->bqhv', o_lat, w_uv_h).reshape(B, L, A_HEADS * A_DV)
    return o, ckv, kpe


def layer(x, c, l, params, cos, sin, m_state, mla_past):
    (w_ada, b_ada, g_pre_mix, g_post_mix, g_pre_ff, g_post_ff, w_in, b_i, b_f, g_mh,
     g_q, w_uq, g_kv, w_uk, w_uv, w_out, w_up, w_down) = params
    ada = (jax.nn.silu(c) @ w_ada[l] + b_ada[l])[:, None, :]
    sh1, sc1, ga1, sh2, sc2, ga2 = jnp.split(ada, 6, axis=-1)
    h = rmsnorm(x, g_pre_mix[l]) * (1.0 + sc1) + sh1
    q, k, v, o_pre, i_pre, f_pre, q_lat, kv_lat, k_r = split_cols(h @ w_in[l])
    C0, n0, m0 = m_state
    ym, C, n, m = mlstm_mixer(q, k, v, o_pre, i_pre, f_pre, b_i[l], b_f[l], g_mh[l], C0, n0, m0)
    if mla_past is None:
        ya, ckv, kpe = mla_prompt(q_lat, kv_lat, k_r, g_q[l], w_uq[l], g_kv[l], w_uk[l], w_uv[l], cos, sin)
    else:
        ya, ckv, kpe = mla_sample(q_lat, kv_lat, k_r, g_q[l], w_uq[l], g_kv[l], w_uk[l], w_uv[l], cos, sin,
                                  mla_past[0], mla_past[1])
    y = jnp.concatenate([ym, ya.astype(ym.dtype)], axis=-1) @ w_out[l]
    x = x + ga1 * rmsnorm(y, g_post_mix[l])
    h = rmsnorm(x, g_pre_ff[l]) * (1.0 + sc2) + sh2
    y = jnp.square(jax.nn.relu(h @ w_up[l])) @ w_down[l]
    x = x + ga2 * rmsnorm(y, g_post_ff[l])
    return x, ckv, kpe, C, n, m


def setup_inputs(seed: int = 0) -> dict:
    key = jax.random.key(seed)
    ks = jax.random.split(key, 40)
    f32 = jnp.float32
    n_pages = PAST_LEN // PAGE_SIZE
    n_pool = (DEC_BATCH * n_pages * 5) // 4
    def nrm(k, shape, scale):
        return jax.random.normal(k, shape, f32) * scale
    def gain(k, n):
        return 1.0 + 0.1 * jax.random.normal(k, (DEPTH, n), f32)
    page_table = jax.random.permutation(ks[9], n_pool)[: DEC_BATCH * n_pages].reshape(DEC_BATCH, n_pages).astype(jnp.int32)
    return {
        "x_prompt": nrm(ks[0], (BATCH, SEQ, D_MODEL), 1.0),
        "x_sample": nrm(ks[1], (DEC_BATCH, DEC_SEQ, D_MODEL), 1.0),
        "c_prompt": nrm(ks[2], (BATCH, D_MODEL), 1.0),
        "c_sample": nrm(ks[3], (DEC_BATCH, D_MODEL), 1.0),
        "cache_kv_latent": nrm(ks[4], (DEPTH, n_pool, PAGE_SIZE, KV_RANK), 1.0),
        "cache_k_rope": nrm(ks[5], (DEPTH, n_pool, PAGE_SIZE, A_ROPE), 1.0),
        "state_C": nrm(ks[6], (DEPTH, DEC_BATCH, M_HEADS, M_DV, M_DK), 1.0),
        "state_n": nrm(ks[7], (DEPTH, DEC_BATCH, M_HEADS, M_DK), 1.0),
        "state_m": nrm(ks[8], (DEPTH, DEC_BATCH, M_HEADS), 1.0),
        "page_table": page_table,
        "w_ada": nrm(ks[10], (DEPTH, D_MODEL, 6 * D_MODEL), 0.5 * D_MODEL ** -0.5),
        "b_ada": nrm(ks[11], (DEPTH, 6 * D_MODEL), 0.01),
        "g_pre_mix": gain(ks[12], D_MODEL),
        "g_post_mix": gain(ks[13], D_MODEL),
        "g_pre_ff": gain(ks[14], D_MODEL),
        "g_post_ff": gain(ks[15], D_MODEL),
        "w_in": nrm(ks[16], (DEPTH, D_MODEL, IN_DIM), D_MODEL ** -0.5),
        "b_i": nrm(ks[17], (DEPTH, M_HEADS), 0.1),
        "b_f": 3.0 + nrm(ks[18], (DEPTH, M_HEADS), 0.1),
        "g_mh": gain(ks[19], M_V_W),
        "g_q": gain(ks[20], Q_RANK),
        "w_uq": nrm(ks[21], (DEPTH, Q_RANK, A_HEADS * (A_NOPE + A_ROPE)), Q_RANK ** -0.5),
        "g_kv": gain(ks[22], KV_RANK),
        "w_uk": nrm(ks[23], (DEPTH, KV_RANK, A_HEADS * A_NOPE), KV_RANK ** -0.5),
        "w_uv": nrm(ks[24], (DEPTH, KV_RANK, A_HEADS * A_DV), KV_RANK ** -0.5),
        "w_out": nrm(ks[25], (DEPTH, MIX_WIDTH, D_MODEL), MIX_WIDTH ** -0.5),
        "w_up": nrm(ks[26], (DEPTH, D_MODEL, D_FF), D_MODEL ** -0.5),
        "w_down": nrm(ks[27], (DEPTH, D_FF, D_MODEL), D_FF ** -0.5),
    }


def reference(x_prompt, x_sample, c_prompt, c_sample, cache_kv_latent, cache_k_rope, state_C, state_n, state_m,
              page_table, w_ada, b_ada, g_pre_mix, g_post_mix, g_pre_ff, g_post_ff, w_in, b_i, b_f, g_mh,
              g_q, w_uq, g_kv, w_uk, w_uv, w_out, w_up, w_down):
    params = (w_ada, b_ada, g_pre_mix, g_post_mix, g_pre_ff, g_post_ff, w_in, b_i, b_f, g_mh,
              g_q, w_uq, g_kv, w_uk, w_uv, w_out, w_up, w_down)
    f32 = jnp.float32
    Bp, Lp, _ = x_prompt.shape
    Bs, Ls, _ = x_sample.shape
    cos_p, sin_p = rope_tables(jnp.arange(Lp, dtype=f32))
    cos_s, sin_s = rope_tables(PAST_LEN + jnp.arange(Ls, dtype=f32))
    n_pages = page_table.shape[1]
    zero_state = (jnp.zeros((Bp, M_HEADS, M_DV, M_DK), f32), jnp.zeros((Bp, M_HEADS, M_DK), f32),
                  jnp.zeros((Bp, M_HEADS), f32))
    xp, xs = x_prompt, x_sample
    kvp, krp, Cp, npr, mp = [], [], [], [], []
    kvs, krs, Cs, nsm, ms = [], [], [], [], []
    for l in range(DEPTH):
        xp, ckv, kpe, C, n, m = layer(xp, c_prompt, l, params, cos_p, sin_p, zero_state, None)
        kvp.append(ckv); krp.append(kpe); Cp.append(C); npr.append(n); mp.append(m)
        past_ckv = cache_kv_latent[l][page_table].reshape(Bs, n_pages * PAGE_SIZE, KV_RANK)
        past_kpe = cache_k_rope[l][page_table].reshape(Bs, n_pages * PAGE_SIZE, A_ROPE)
        xs, ckv, kpe, C, n, m = layer(xs, c_sample, l, params, cos_s, sin_s,
                                      (state_C[l], state_n[l], state_m[l]), (past_ckv, past_kpe))
        kvs.append(ckv); krs.append(kpe); Cs.append(C); nsm.append(n); ms.append(m)
    return (xp, xs,
            jnp.stack(kvp), jnp.stack(krp), jnp.stack(Cp), jnp.stack(npr), jnp.stack(mp),
            jnp.stack(kvs), jnp.stack(krs), jnp.stack(Cs), jnp.stack(nsm), jnp.stack(ms))
```

```python
import functools

import jax
import jax.numpy as jnp
import numpy as np
from jax import lax
from jax.experimental import pallas as pl
from jax.experimental.pallas import tpu as pltpu

F32 = jnp.float32
BF16 = jnp.bfloat16

EPS = 1e-6
M_HEADS = 4
M_DK = 128
M_DV = 256
M_CHUNK = 128
A_HEADS = 8
A_DV = 128
A_NOPE = 128
A_ROPE = 64
Q_RANK = 384
KV_RANK = 256
ROPE_THETA = 10000.0
PAST_LEN = 16384
PAGE_SIZE = 128
LANES = 128

U_WIDTH = 4096
U_Q, U_K, U_V, U_O = 0, 512, 1024, 2048
U_QLAT = 3072
U_GATES = 3456
U_KVLAT = 3584
U_KR = 3840

V7X_VMEM_LIMIT = 56 * 1024 * 1024
MLA_SCALE = float((A_NOPE + A_ROPE) ** -0.5)
MLSTM_SCALE = float(M_DK ** -0.5)
PAGES_PER_STEP = 16

_NT = (((1,), (1,)), ((), ()))


def _params(semantics, vmem=V7X_VMEM_LIMIT):
    return pltpu.CompilerParams(dimension_semantics=semantics, vmem_limit_bytes=vmem)


def _rms(x):
    return x * lax.rsqrt(jnp.mean(x * x, axis=-1, keepdims=True) + EPS)


def _log_sigmoid(x):
    return jnp.minimum(x, 0.0) - jnp.log1p(jnp.exp(-jnp.abs(x)))


def _swap_halves(x):
    return jnp.concatenate([x[:, 64:], x[:, :64]], axis=1)


def _ada_kernel(c_ref, w_ref, b_ref, o_ref):
    c = c_ref[...]
    a = (c * jax.nn.sigmoid(c)).astype(BF16)
    o_ref[...] = jnp.dot(a, w_ref[...].astype(BF16), preferred_element_type=F32) + b_ref[...]


def _ada(c_all, w_ada, b_ada):
    depth, d, n = w_ada.shape
    r = c_all.shape[0]
    tn = 1024
    return pl.pallas_call(
        _ada_kernel,
        grid=(depth, n // tn),
        in_specs=[
            pl.BlockSpec((r, d), lambda l, j: (0, 0)),
            pl.BlockSpec((None, d, tn), lambda l, j: (l, 0, j)),
            pl.BlockSpec((None, 1, tn), lambda l, j: (l, 0, j)),
        ],
        out_specs=pl.BlockSpec((None, r, tn), lambda l, j: (l, 0, j)),
        out_shape=jax.ShapeDtypeStruct((depth, r, n), F32),
        compiler_params=_params(("arbitrary", "arbitrary")),
        name="ada",
    )(c_all, w_ada, b_ada.reshape(depth, 1, n))


def _in_proj_kernel(x_ref, g_ref, sc_ref, sh_ref, w_ref, o_ref, h_ref):
    @pl.when(pl.program_id(1) == 0)
    def _():
        h = (_rms(x_ref[...]) * g_ref[...]) * (1.0 + sc_ref[...]) + sh_ref[...]
        h_ref[...] = h.astype(BF16)

    o_ref[...] = jnp.dot(h_ref[...], w_ref[...], preferred_element_type=F32)


def _mod_spec(mod, tm, t):
    g, r, d = mod.shape
    rows_per_group = t // g
    assert r in (1, tm) and rows_per_group % tm == 0
    return pl.BlockSpec((None, r, d), lambda i, *_: ((i * tm) // rows_per_group, 0, 0))


def _in_proj(x, g, sc, sh, w):
    t, d = x.shape
    n = w.shape[1]
    tm = min(1024, t)
    tn = 1024
    return pl.pallas_call(
        _in_proj_kernel,
        grid=(t // tm, n // tn),
        in_specs=[
            pl.BlockSpec((tm, d), lambda i, j: (i, 0)),
            pl.BlockSpec((1, d), lambda i, j: (0, 0)),
            _mod_spec(sc, tm, t),
            _mod_spec(sh, tm, t),
            pl.BlockSpec((d, tn), lambda i, j: (0, j)),
        ],
        out_specs=pl.BlockSpec((tm, tn), lambda i, j: (i, j)),
        out_shape=jax.ShapeDtypeStruct((t, n), F32),
        scratch_shapes=[pltpu.VMEM((tm, d), BF16)],
        compiler_params=_params(("arbitrary", "arbitrary")),
        name="in_proj",
    )(x, g, sc, sh, w)


def _head_out(hh, gmh, o_pre):
    return ((_rms(hh) * gmh) * jax.nn.sigmoid(o_pre)).astype(BF16)


def _mlstm_chunk_kernel(bi_ref, bf_ref, q_ref, k_ref, v_ref, o_ref, g_ref, gmh_ref,
                        ym_ref, cout_ref, nout_ref, mout_ref, ct_ref, n_ref, m_ref):
    c = pl.program_id(1)
    cs = M_CHUNK

    @pl.when(c == 0)
    def _():
        ct_ref[...] = jnp.zeros_like(ct_ref)
        n_ref[...] = jnp.zeros_like(n_ref)
        m_ref[...] = jnp.zeros_like(m_ref)

    g = g_ref[...]
    gt = g.T
    row = lax.broadcasted_iota(jnp.int32, (cs, cs), 0)
    col = lax.broadcasted_iota(jnp.int32, (cs, cs), 1)
    causal = col <= row

    for h in range(M_HEADS):
        i_col = g[:, h:h + 1] + bi_ref[h]
        f_col = _log_sigmoid(g[:, M_HEADS + h:M_HEADS + h + 1] + bf_ref[h])
        i_row = gt[h:h + 1, :] + bi_ref[h]
        f_row = _log_sigmoid(gt[M_HEADS + h:M_HEADS + h + 1, :] + bf_ref[h])
        b_col = jnp.sum(jnp.where(causal, f_row, 0.0), axis=1, keepdims=True)
        b_row = jnp.sum(jnp.where(row <= col, f_col, 0.0), axis=0, keepdims=True)
        m_prev = m_ref[h][:, :1]

        dm = jnp.where(causal, b_col - b_row + i_row, -jnp.inf)
        log_inter = b_col + m_prev
        m_t = jnp.maximum(log_inter, jnp.max(dm, axis=1, keepdims=True))
        w_inter = jnp.exp(log_inter - m_t) * MLSTM_SCALE
        p = jnp.exp(dm - m_t)

        q = q_ref[:, h * M_DK:(h + 1) * M_DK]
        k = k_ref[:, h * M_DK:(h + 1) * M_DK]
        v = v_ref[:, h * M_DV:(h + 1) * M_DV]
        qb = q.astype(BF16)
        ktb = k.T.astype(BF16)
        ct = ct_ref[h]
        n = n_ref[h]

        s = jnp.dot(qb, ktb, preferred_element_type=F32) * (p * MLSTM_SCALE)
        num = w_inter * jnp.dot(qb, ct.astype(BF16), preferred_element_type=F32) \
            + jnp.dot(s.astype(BF16), v.astype(BF16), preferred_element_type=F32)
        den = w_inter * jnp.sum(q * n, axis=1, keepdims=True) + jnp.sum(s, axis=1, keepdims=True)
        hh = num / jnp.maximum(jnp.abs(den), jnp.exp(-m_t))

        m_new = m_t[cs - 1:cs, :]
        b_last = b_col[cs - 1:cs, :]
        a_inter = jnp.exp(b_last + m_prev - m_new)
        a_col = jnp.exp(b_last - b_col + i_col - m_new)
        ct_ref[h] = a_inter * ct + jnp.dot(ktb, (v * a_col).astype(BF16), preferred_element_type=F32)
        n_ref[h] = a_inter * n + jnp.sum(a_col * k, axis=0, keepdims=True)
        m_ref[h] = jnp.broadcast_to(m_new, (1, LANES))

        sl = slice(h * M_DV, (h + 1) * M_DV)
        ym_ref[:, sl] = _head_out(hh, gmh_ref[:, sl], o_ref[:, sl])

    @pl.when(c == pl.num_programs(1) - 1)
    def _():
        for h in range(M_HEADS):
            cout_ref[h] = ct_ref[h].T
            nout_ref[h:h + 1, :] = n_ref[h]
            mout_ref[h:h + 1, :] = m_ref[h]


def _mlstm_prompt(u, b_i, b_f, g_mh, batch, seq):
    t = u.shape[0]
    nc = seq // M_CHUNK
    cs = M_CHUNK
    qw, vw = M_HEADS * M_DK, M_HEADS * M_DV
    rowmap = lambda b, c: b * nc + c
    smem = pl.BlockSpec(memory_space=pltpu.SMEM)
    return pl.pallas_call(
        _mlstm_chunk_kernel,
        grid=(batch, nc),
        in_specs=[
            smem, smem,
            pl.BlockSpec((cs, qw), lambda b, c: (rowmap(b, c), U_Q // qw)),
            pl.BlockSpec((cs, qw), lambda b, c: (rowmap(b, c), U_K // qw)),
            pl.BlockSpec((cs, vw), lambda b, c: (rowmap(b, c), U_V // vw)),
            pl.BlockSpec((cs, vw), lambda b, c: (rowmap(b, c), U_O // vw)),
            pl.BlockSpec((cs, LANES), lambda b, c: (rowmap(b, c), U_GATES // LANES)),
            pl.BlockSpec((1, vw), lambda b, c: (0, 0)),
        ],
        out_specs=[
            pl.BlockSpec((cs, vw), lambda b, c: (rowmap(b, c), 0)),
            pl.BlockSpec((None, M_HEADS, M_DV, M_DK), lambda b, c: (b, 0, 0, 0)),
            pl.BlockSpec((None, M_HEADS, M_DK), lambda b, c: (b, 0, 0)),
            pl.BlockSpec((None, M_HEADS, LANES), lambda b, c: (b, 0, 0)),
        ],
        out_shape=[
            jax.ShapeDtypeStruct((t, vw), BF16),
            jax.ShapeDtypeStruct((batch, M_HEADS, M_DV, M_DK), F32),
            jax.ShapeDtypeStruct((batch, M_HEADS, M_DK), F32),
            jax.ShapeDtypeStruct((batch, M_HEADS, LANES), F32),
        ],
        scratch_shapes=[
            pltpu.VMEM((M_HEADS, M_DK, M_DV), F32),
            pltpu.VMEM((M_HEADS, 1, M_DK), F32),
            pltpu.VMEM((M_HEADS, 1, LANES), F32),
        ],
        compiler_params=_params(("arbitrary", "arbitrary")),
        name="mlstm_prompt",
    )(b_i, b_f, u, u, u, u, u, g_mh)


def _mlstm_step_kernel(bi_ref, bf_ref, q_ref, k_ref, v_ref, o_ref, g_ref, gmh_ref, c0_ref, n0_ref, m0_ref,
                       ym_ref, cout_ref, nout_ref, mout_ref):
    g = g_ref[...]
    eye = (lax.broadcasted_iota(jnp.int32, (M_DV, M_DV), 0)
           == lax.broadcasted_iota(jnp.int32, (M_DV, M_DV), 1))
    for h in range(M_HEADS):
        ig = g[:, h:h + 1] + bi_ref[h]
        lf = _log_sigmoid(g[:, M_HEADS + h:M_HEADS + h + 1] + bf_ref[h])
        m0 = m0_ref[h:h + 1, :1]
        log_inter = lf + m0
        m_t = jnp.maximum(log_inter, ig)
        w_inter = jnp.exp(log_inter - m_t)
        p = jnp.exp(ig - m_t)

        q = q_ref[:, h * M_DK:(h + 1) * M_DK]
        k = k_ref[:, h * M_DK:(h + 1) * M_DK]
        v = v_ref[:, h * M_DV:(h + 1) * M_DV]
        c0 = c0_ref[h]
        n0 = n0_ref[h:h + 1, :]

        s = jnp.sum(q * k, axis=1, keepdims=True) * (p * MLSTM_SCALE)
        q8 = jnp.broadcast_to(q, (8, M_DK)).astype(BF16)
        cq = lax.dot_general(q8, c0.astype(BF16), _NT, preferred_element_type=F32)[:1, :]
        wq = w_inter * MLSTM_SCALE
        num = wq * cq + s * v
        den = wq * jnp.sum(n0 * q, axis=1, keepdims=True) + s
        hh = num / jnp.maximum(jnp.abs(den), jnp.exp(-m_t))

        v_col = jnp.sum(jnp.where(eye, jnp.broadcast_to(v, (M_DV, M_DV)), 0.0), axis=1, keepdims=True)
        cout_ref[h] = w_inter * c0 + (p * v_col) * k
        nout_ref[h:h + 1, :] = w_inter * n0 + p * k
        mout_ref[h:h + 1, :] = jnp.broadcast_to(m_t, (1, LANES))

        sl = slice(h * M_DV, (h + 1) * M_DV)
        ym_ref[:, sl] = _head_out(hh, gmh_ref[:, sl], o_ref[:, sl])


def _mlstm_sample(u, b_i, b_f, g_mh, c0, n0, m0):
    batch = u.shape[0]
    qw, vw = M_HEADS * M_DK, M_HEADS * M_DV
    u3 = u.reshape(batch, 1, U_WIDTH)
    smem = pl.BlockSpec(memory_space=pltpu.SMEM)
    ym, c, n, m = pl.pallas_call(
        _mlstm_step_kernel,
        grid=(batch,),
        in_specs=[
            smem, smem,
            pl.BlockSpec((None, 1, qw), lambda b: (b, 0, U_Q // qw)),
            pl.BlockSpec((None, 1, qw), lambda b: (b, 0, U_K // qw)),
            pl.BlockSpec((None, 1, vw), lambda b: (b, 0, U_V // vw)),
            pl.BlockSpec((None, 1, vw), lambda b: (b, 0, U_O // vw)),
            pl.BlockSpec((None, 1, LANES), lambda b: (b, 0, U_GATES // LANES)),
            pl.BlockSpec((1, vw), lambda b: (0, 0)),
            pl.BlockSpec((None, M_HEADS, M_DV, M_DK), lambda b: (b, 0, 0, 0)),
            pl.BlockSpec((None, M_HEADS, M_DK), lambda b: (b, 0, 0)),
            pl.BlockSpec((None, M_HEADS, LANES), lambda b: (b, 0, 0)),
        ],
        out_specs=[
            pl.BlockSpec((None, 1, vw), lambda b: (b, 0, 0)),
            pl.BlockSpec((None, M_HEADS, M_DV, M_DK), lambda b: (b, 0, 0, 0)),
            pl.BlockSpec((None, M_HEADS, M_DK), lambda b: (b, 0, 0)),
            pl.BlockSpec((None, M_HEADS, LANES), lambda b: (b, 0, 0)),
        ],
        out_shape=[
            jax.ShapeDtypeStruct((batch, 1, vw), BF16),
            jax.ShapeDtypeStruct((batch, M_HEADS, M_DV, M_DK), F32),
            jax.ShapeDtypeStruct((batch, M_HEADS, M_DK), F32),
            jax.ShapeDtypeStruct((batch, M_HEADS, LANES), F32),
        ],
        compiler_params=_params(("arbitrary",)),
        name="mlstm_sample",
    )(b_i, b_f, u3, u3, u3, u3, u3, g_mh, c0, n0, m0)
    return ym.reshape(batch, vw), c, n, m


def _mla_common(ql_ref, kvl_ref, kr_ref, gq_ref, gkv_ref, wq_ref, cos_ref, sin_ref,
                qcat_ref, ckv_ref, kpe_ref):
    cos = cos_ref[...]
    sin = sin_ref[...]
    hq = (_rms(ql_ref[...]) * gq_ref[...]).astype(BF16)
    q = jnp.dot(hq, wq_ref[...], preferred_element_type=F32)
    for h in range(A_HEADS):
        base = h * 2 * LANES
        qr = q[:, base + LANES:base + 2 * LANES]
        qcat_ref[:, base:base + LANES] = q[:, base:base + LANES].astype(BF16)
        qcat_ref[:, base + LANES:base + 2 * LANES] = (qr * cos + _swap_halves(qr) * sin).astype(BF16)
    ckv = _rms(kvl_ref[...]) * gkv_ref[...]
    ckv_ref[...] = ckv
    kr = kr_ref[...]
    kpe128 = kr * cos + _swap_halves(kr) * sin
    kpe_ref[...] = kpe128[:, :A_ROPE]
    return q, ckv, kpe128


def _mla_proj_prompt_kernel(ql_ref, kvl_ref, kr_ref, gq_ref, gkv_ref, wq_ref, wuk_ref, wuv_ref, cos_ref, sin_ref,
                            qcat_ref, ckv_ref, kpe_ref, kn_ref, v_ref, kpe128_ref):
    _, ckv, kpe128 = _mla_common(ql_ref, kvl_ref, kr_ref, gq_ref, gkv_ref, wq_ref, cos_ref, sin_ref,
                                 qcat_ref, ckv_ref, kpe_ref)
    cb = ckv.astype(BF16)
    kn_ref[...] = jnp.dot(cb, wuk_ref[...], preferred_element_type=F32).astype(BF16)
    v_ref[...] = jnp.dot(cb, wuv_ref[...], preferred_element_type=F32).astype(BF16)
    kpe128_ref[...] = kpe128.astype(BF16)


def _mla_proj_sample_kernel(ql_ref, kvl_ref, kr_ref, gq_ref, gkv_ref, wq_ref, wuk_ref, cos_ref, sin_ref,
                            qcat_ref, ckv_ref, kpe_ref, qabs_ref):
    q, _, _ = _mla_common(ql_ref, kvl_ref, kr_ref, gq_ref, gkv_ref, wq_ref, cos_ref, sin_ref,
                          qcat_ref, ckv_ref, kpe_ref)
    for h in range(A_HEADS):
        qn = q[:, h * 2 * LANES:h * 2 * LANES + A_NOPE].astype(BF16)
        wk = wuk_ref[:, h * A_NOPE:(h + 1) * A_NOPE]
        qabs_ref[:, h * KV_RANK:(h + 1) * KV_RANK] = lax.dot_general(
            qn, wk, _NT, preferred_element_type=F32).astype(BF16)


def _mla_proj(u, g_q, g_kv, wq, wuk, wuv, cos, sin, *, prompt):
    t = u.shape[0]
    tm = min(512, t)
    npos = cos.shape[0] // tm
    hw = A_HEADS * 2 * LANES
    full = lambda a: pl.BlockSpec(a.shape, lambda i: (0, 0))
    in_specs = [
        pl.BlockSpec((tm, Q_RANK), lambda i: (i, U_QLAT // Q_RANK)),
        pl.BlockSpec((tm, KV_RANK), lambda i: (i, U_KVLAT // KV_RANK)),
        pl.BlockSpec((tm, LANES), lambda i: (i, U_KR // LANES)),
        full(g_q), full(g_kv), full(wq), full(wuk),
    ]
    args = [u, u, u, g_q, g_kv, wq, wuk]
    if prompt:
        in_specs.append(full(wuv))
        args.append(wuv)
    in_specs += [pl.BlockSpec((tm, LANES), lambda i: (i % npos, 0))] * 2
    args += [cos, sin]
    row = lambda w: pl.BlockSpec((tm, w), lambda i: (i, 0))
    out_specs = [row(hw), row(KV_RANK), row(A_ROPE)]
    out_shape = [jax.ShapeDtypeStruct((t, hw), BF16), jax.ShapeDtypeStruct((t, KV_RANK), F32),
                 jax.ShapeDtypeStruct((t, A_ROPE), F32)]
    if prompt:
        out_specs += [row(A_HEADS * A_NOPE), row(A_HEADS * A_DV), row(LANES)]
        out_shape += [jax.ShapeDtypeStruct((t, A_HEADS * A_NOPE), BF16),
                      jax.ShapeDtypeStruct((t, A_HEADS * A_DV), BF16),
                      jax.ShapeDtypeStruct((t, LANES), BF16)]
        body = _mla_proj_prompt_kernel
    else:
        out_specs += [row(A_HEADS * KV_RANK)]
        out_shape += [jax.ShapeDtypeStruct((t, A_HEADS * KV_RANK), BF16)]
        body = _mla_proj_sample_kernel
    return pl.pallas_call(
        body,
        grid=(t // tm,),
        in_specs=in_specs,
        out_specs=out_specs,
        out_shape=out_shape,
        compiler_params=_params(("arbitrary",)),
        name="mla_proj_prompt" if prompt else "mla_proj_sample",
    )(*args)


def _flash_kernel(q_ref, kn_ref, kpe_ref, v_ref, o_ref, m_ref, l_ref, acc_ref, *, tile):
    qi = pl.program_id(2)
    q = q_ref[...]
    m_ref[...] = jnp.full_like(m_ref, -jnp.inf)
    l_ref[...] = jnp.zeros_like(l_ref)
    acc_ref[...] = jnp.zeros_like(acc_ref)

    def step(ki, diagonal):
        off = pl.multiple_of(ki * tile, tile)
        k = jnp.concatenate([kn_ref[pl.ds(off, tile), :], kpe_ref[pl.ds(off, tile), :]], axis=1)
        s = lax.dot_general(q, k, _NT, preferred_element_type=F32) * MLA_SCALE
        if diagonal:
            row = lax.broadcasted_iota(jnp.int32, (tile, tile), 0)
            col = lax.broadcasted_iota(jnp.int32, (tile, tile), 1)
            s = jnp.where(col <= row, s, -jnp.inf)
        m_prev = m_ref[...]
        m_new = jnp.maximum(m_prev, jnp.max(s, axis=1, keepdims=True))
        alpha = jnp.exp(m_prev - m_new)
        p = jnp.exp(s - m_new)
        l_ref[...] = alpha * l_ref[...] + jnp.sum(p, axis=1, keepdims=True)
        acc_ref[...] = alpha * acc_ref[...] + jnp.dot(p.astype(BF16), v_ref[pl.ds(off, tile), :],
                                                      preferred_element_type=F32)
        m_ref[...] = m_new

    def body(ki, carry):
        step(ki, False)
        return carry

    lax.fori_loop(0, qi, body, 0)
    step(qi, True)
    o_ref[...] = (acc_ref[...] / l_ref[...]).astype(BF16)


def _flash(qcat, kn, kpe128, v, batch, seq):
    t = qcat.shape[0]
    tile = min(512, seq)
    nq = seq // tile
    return pl.pallas_call(
        functools.partial(_flash_kernel, tile=tile),
        grid=(batch, A_HEADS, nq),
        in_specs=[
            pl.BlockSpec((tile, 2 * LANES), lambda b, h, i: (b * nq + i, h)),
            pl.BlockSpec((seq, A_NOPE), lambda b, h, i: (b, h)),
            pl.BlockSpec((seq, LANES), lambda b, h, i: (b, 0)),
            pl.BlockSpec((seq, A_DV), lambda b, h, i: (b, h)),
        ],
        out_specs=pl.BlockSpec((tile, A_DV), lambda b, h, i: (b * nq + i, h)),
        out_shape=jax.ShapeDtypeStruct((t, A_HEADS * A_DV), BF16),
        scratch_shapes=[pltpu.VMEM((tile, 1), F32), pltpu.VMEM((tile, 1), F32), pltpu.VMEM((tile, A_DV), F32)],
        compiler_params=_params(("arbitrary", "arbitrary", "arbitrary")),
        name="mla_flash",
    )(qcat, kn, kpe128, v)


def _decode_kernel(pt_ref, qabs_ref, qcat_ref, ckvn_ref, kpen_ref, *rest, pages):
    kv_refs = rest[:pages]
    kr_refs = rest[pages:2 * pages]
    o_ref, m_ref, l_ref, acc_ref = rest[2 * pages:]
    j = pl.program_id(1)

    @pl.when(j == 0)
    def _():
        m_ref[...] = jnp.full_like(m_ref, -jnp.inf)
        l_ref[...] = jnp.zeros_like(l_ref)
        acc_ref[...] = jnp.zeros_like(acc_ref)

    qa = qabs_ref[...]
    qp = qcat_ref[:, A_NOPE:A_NOPE + A_ROPE]
    kvs = [r[...].astype(BF16) for r in kv_refs]
    s = jnp.concatenate(
        [lax.dot_general(qa, kvs[i], _NT, preferred_element_type=F32)
         + lax.dot_general(qp, kr_refs[i][...].astype(BF16), _NT, preferred_element_type=F32)
         for i in range(pages)], axis=1) * MLA_SCALE
    m_prev = m_ref[...]
    m_new = jnp.maximum(m_prev, jnp.max(s, axis=1, keepdims=True))
    alpha = jnp.exp(m_prev - m_new)
    p = jnp.exp(s - m_new)
    l_new = alpha * l_ref[...] + jnp.sum(p, axis=1, keepdims=True)
    pb = p.astype(BF16)
    pv = jnp.dot(pb[:, :PAGE_SIZE], kvs[0], preferred_element_type=F32)
    for i in range(1, pages):
        pv += jnp.dot(pb[:, i * PAGE_SIZE:(i + 1) * PAGE_SIZE], kvs[i], preferred_element_type=F32)
    acc_new = alpha * acc_ref[...] + pv
    m_ref[...] = m_new
    l_ref[...] = l_new
    acc_ref[...] = acc_new

    @pl.when(j == pl.num_programs(1) - 1)
    def _():
        ckv_n = ckvn_ref[...]
        kpe_n = kpen_ref[...]
        s_n = (jnp.sum(qa.astype(F32) * ckv_n, axis=1, keepdims=True)
               + jnp.sum(qp.astype(F32) * kpe_n, axis=1, keepdims=True)) * MLA_SCALE
        m_f = jnp.maximum(m_new, s_n)
        a_past = jnp.exp(m_new - m_f)
        a_new = jnp.exp(s_n - m_f)
        o_ref[...] = (acc_new * a_past + a_new * ckv_n) / (l_new * a_past + a_new)


def _decode_attention(page_table, qabs, qcat, ckv_new, kpe_new, cache_kv, cache_kr, layer):
    batch, n_pages = page_table.shape
    pages = PAGES_PER_STEP
    assert n_pages % pages == 0
    qabs3 = qabs.reshape(batch, A_HEADS, KV_RANK)
    qcat3 = qcat.reshape(batch, A_HEADS, 2 * LANES)

    def page_spec(width, i):
        return pl.BlockSpec((None, None, PAGE_SIZE, width),
                            lambda b, j, pt: (layer, pt[b, j * pages + i], 0, 0))

    per_b = lambda shape: pl.BlockSpec((None,) + shape, lambda b, j, pt: (b, 0, 0))
    grid_spec = pltpu.PrefetchScalarGridSpec(
        num_scalar_prefetch=1,
        grid=(batch, n_pages // pages),
        in_specs=[per_b((A_HEADS, KV_RANK)), per_b((A_HEADS, 2 * LANES)), per_b((1, KV_RANK)), per_b((1, A_ROPE))]
        + [page_spec(KV_RANK, i) for i in range(pages)]
        + [page_spec(A_ROPE, i) for i in range(pages)],
        out_specs=per_b((A_HEADS, KV_RANK)),
        scratch_shapes=[pltpu.VMEM((A_HEADS, 1), F32), pltpu.VMEM((A_HEADS, 1), F32),
                        pltpu.VMEM((A_HEADS, KV_RANK), F32)],
    )
    o_lat = pl.pallas_call(
        functools.partial(_decode_kernel, pages=pages),
        grid_spec=grid_spec,
        out_shape=jax.ShapeDtypeStruct((batch, A_HEADS, KV_RANK), F32),
        compiler_params=_params(("arbitrary", "arbitrary")),
        name="mla_decode",
    )(page_table, qabs3, qcat3, ckv_new.reshape(batch, 1, KV_RANK), kpe_new.reshape(batch, 1, A_ROPE),
      *([cache_kv] * pages), *([cache_kr] * pages))
    return o_lat.reshape(batch, A_HEADS * KV_RANK)


def _latent_out_kernel(o_ref, wuv_ref, ya_ref):
    for h in range(A_HEADS):
        ya_ref[:, h * A_DV:(h + 1) * A_DV] = jnp.dot(
            o_ref[:, h * KV_RANK:(h + 1) * KV_RANK].astype(BF16), wuv_ref[:, h * A_DV:(h + 1) * A_DV],
            preferred_element_type=F32).astype(BF16)


def _latent_out(o_lat, wuv):
    batch = o_lat.shape[0]
    return pl.pallas_call(
        _latent_out_kernel,
        out_shape=jax.ShapeDtypeStruct((batch, A_HEADS * A_DV), BF16),
        name="mla_latent_out",
    )(o_lat, wuv)


def _out_proj_kernel(ym_ref, ya_ref, x_ref, w1_ref, w2_ref, gpost_ref, ga_ref, gpre_ref, sc_ref, sh_ref,
                     xo_ref, h_ref):
    y = jnp.dot(ym_ref[...], w1_ref[...], preferred_element_type=F32) \
        + jnp.dot(ya_ref[...], w2_ref[...], preferred_element_type=F32)
    xn = x_ref[...] + ga_ref[...] * (_rms(y) * gpost_ref[...])
    xo_ref[...] = xn
    h_ref[...] = ((_rms(xn) * gpre_ref[...]) * (1.0 + sc_ref[...]) + sh_ref[...]).astype(BF16)


def _out_proj(ym, ya, x, w_out, g_post, ga, g_pre, sc, sh):
    t, d = x.shape
    half = ym.shape[1]
    tm = min(256, t)
    vec = pl.BlockSpec((1, d), lambda i: (0, 0))
    return pl.pallas_call(
        _out_proj_kernel,
        grid=(t // tm,),
        in_specs=[
            pl.BlockSpec((tm, half), lambda i: (i, 0)),
            pl.BlockSpec((tm, half), lambda i: (i, 0)),
            pl.BlockSpec((tm, d), lambda i: (i, 0)),
            pl.BlockSpec((half, d), lambda i: (0, 0)),
            pl.BlockSpec((half, d), lambda i: (1, 0)),
            vec, _mod_spec(ga, tm, t), vec, _mod_spec(sc, tm, t), _mod_spec(sh, tm, t),
        ],
        out_specs=[pl.BlockSpec((tm, d), lambda i: (i, 0)), pl.BlockSpec((tm, d), lambda i: (i, 0))],
        out_shape=[jax.ShapeDtypeStruct((t, d), F32), jax.ShapeDtypeStruct((t, d), BF16)],
        compiler_params=_params(("arbitrary",)),
        name="out_proj",
    )(ym, ya, x, w_out, w_out, g_post, ga, g_pre, sc, sh)


def _mlp_kernel(h_ref, wu_ref, wd_ref, x_ref, gpost_ref, ga_ref, xo_ref, acc_ref):
    f = pl.program_id(1)

    @pl.when(f == 0)
    def _():
        acc_ref[...] = jnp.zeros_like(acc_ref)

    a = jnp.dot(h_ref[...], wu_ref[...], preferred_element_type=F32)
    a = jnp.square(jnp.maximum(a, 0.0)).astype(BF16)
    acc_ref[...] += jnp.dot(a, wd_ref[...], preferred_element_type=F32)

    @pl.when(f == pl.num_programs(1) - 1)
    def _():
        xo_ref[...] = x_ref[...] + ga_ref[...] * (_rms(acc_ref[...]) * gpost_ref[...])


def _mlp(h, x, w_up, w_down, g_post, ga):
    t, d = x.shape
    ff = w_up.shape[1]
    tm = min(512, t)
    tf = 1024
    return pl.pallas_call(
        _mlp_kernel,
        grid=(t // tm, ff // tf),
        in_specs=[
            pl.BlockSpec((tm, d), lambda i, f: (i, 0)),
            pl.BlockSpec((d, tf), lambda i, f: (0, f)),
            pl.BlockSpec((tf, d), lambda i, f: (f, 0)),
            pl.BlockSpec((tm, d), lambda i, f: (i, 0)),
            pl.BlockSpec((1, d), lambda i, f: (0, 0)),
            _mod_spec(ga, tm, t),
        ],
        out_specs=pl.BlockSpec((tm, d), lambda i, f: (i, 0)),
        out_shape=jax.ShapeDtypeStruct((t, d), F32),
        scratch_shapes=[pltpu.VMEM((tm, d), F32)],
        compiler_params=_params(("arbitrary", "arbitrary")),
        name="mlp",
    )(h, w_up, w_down, x, g_post, ga)


def _prep_w_in(w_in):
    qw, vw = M_HEADS * M_DK, M_HEADS * M_DV
    o = 2 * qw + 2 * vw
    gates = w_in[..., o:o + 2 * M_HEADS]
    o += 2 * M_HEADS
    q_lat = w_in[..., o:o + Q_RANK]
    o += Q_RANK
    kv_lat = w_in[..., o:o + KV_RANK]
    o += KV_RANK
    k_r = w_in[..., o:o + A_ROPE]
    half = A_ROPE // 2
    zeros = lambda n: jnp.zeros(w_in.shape[:-1] + (n,), w_in.dtype)
    out = jnp.concatenate([
        w_in[..., :2 * qw + 2 * vw],
        q_lat,
        gates, zeros(LANES - 2 * M_HEADS),
        kv_lat,
        k_r, k_r[..., half:], k_r[..., :half],
        zeros(U_WIDTH - U_KR - LANES),
    ], axis=-1)
    assert out.shape[-1] == U_WIDTH
    return out.astype(BF16)


def _prep_w_uq(w_uq):
    depth, r, _ = w_uq.shape
    w = w_uq.reshape(depth, r, A_HEADS, A_NOPE + A_ROPE)
    half = A_ROPE // 2
    x1 = w[..., A_NOPE:A_NOPE + half]
    x2 = w[..., A_NOPE + half:]
    out = jnp.concatenate([w[..., :A_NOPE], x1, x2, x2, x1], axis=-1)
    return out.reshape(depth, r, A_HEADS * 2 * LANES).astype(BF16)


def _rope_tables(pos):
    freqs = ROPE_THETA ** (-jnp.arange(0, A_ROPE, 2, dtype=F32) / A_ROPE)
    ang = pos[:, None] * freqs[None, :]
    cos, sin = jnp.cos(ang), jnp.sin(ang)
    z = jnp.zeros((pos.shape[0], A_ROPE), F32)
    return jnp.concatenate([cos, cos, z], axis=1), jnp.concatenate([-sin, sin, z], axis=1)


def _split_mod(ada, groups, rows):
    d = ada.shape[1] // 6
    return [ada[:, i * d:(i + 1) * d].reshape(groups, rows, d) for i in range(6)]


def kernel(x_prompt, x_sample, c_prompt, c_sample, cache_kv_latent, cache_k_rope, state_C, state_n, state_m,
           page_table, w_ada, b_ada, g_pre_mix, g_post_mix, g_pre_ff, g_post_ff, w_in, b_i, b_f, g_mh,
           g_q, w_uq, g_kv, w_uk, w_uv, w_out, w_up, w_down):
    bp, lp, d = x_prompt.shape
    bs, ls, _ = x_sample.shape
    assert ls == 1
    depth = w_in.shape[0]
    tp = bp * lp

    w_in_b = _prep_w_in(w_in)
    w_uq_b = _prep_w_uq(w_uq)
    w_uk_b = w_uk.astype(BF16)
    w_uv_b = w_uv.astype(BF16)
    w_out_b = w_out.astype(BF16)
    w_up_b = w_up.astype(BF16)
    w_down_b = w_down.astype(BF16)

    pad = (-(bp + bs)) % 8
    c_all = jnp.concatenate([c_prompt, c_sample, jnp.zeros((pad, d), F32)], axis=0)
    ada = _ada(c_all, w_ada, b_ada)

    cos_p, sin_p = _rope_tables(jnp.arange(lp, dtype=F32))
    cos_s, sin_s = _rope_tables(jnp.full((bs,), PAST_LEN, F32) + jnp.arange(ls, dtype=F32))

    xp = x_prompt.reshape(tp, d)
    xs = x_sample.reshape(bs, d)
    outs_p = [[] for _ in range(5)]
    outs_s = [[] for _ in range(5)]
    vec = lambda a: a.reshape(1, -1)

    for l in range(depth):
        gains = dict(g_pre=vec(g_pre_mix[l]), g_post=vec(g_post_mix[l]), g_pre_ff=vec(g_pre_ff[l]),
                     g_post_ff=vec(g_post_ff[l]), g_mh=vec(g_mh[l]), g_q=vec(g_q[l]), g_kv=vec(g_kv[l]))

        sh1, sc1, ga1, sh2, sc2, ga2 = _split_mod(ada[l, :bp], bp, 1)
        u = _in_proj(xp, gains["g_pre"], sc1, sh1, w_in_b[l])
        ym, c_new, n_new, m_new = _mlstm_prompt(u, b_i[l], b_f[l], gains["g_mh"], bp, lp)
        qcat, ckv, kpe, kn, v, kpe128 = _mla_proj(u, gains["g_q"], gains["g_kv"], w_uq_b[l], w_uk_b[l], w_uv_b[l],
                                                  cos_p, sin_p, prompt=True)
        ya = _flash(qcat, kn, kpe128, v, bp, lp)
        xp, h2 = _out_proj(ym, ya, xp, w_out_b[l], gains["g_post"], ga1, gains["g_pre_ff"], sc2, sh2)
        xp = _mlp(h2, xp, w_up_b[l], w_down_b[l], gains["g_post_ff"], ga2)
        for acc, val in zip(outs_p, (ckv.reshape(bp, lp, KV_RANK), kpe.reshape(bp, lp, A_ROPE),
                                     c_new, n_new, m_new[:, :, 0])):
            acc.append(val)

        sh1, sc1, ga1, sh2, sc2, ga2 = _split_mod(ada[l, bp:bp + bs], 1, bs)
        u = _in_proj(xs, gains["g_pre"], sc1, sh1, w_in_b[l])
        m0 = jnp.broadcast_to(state_m[l][:, :, None], (bs, M_HEADS, LANES))
        ym, c_new, n_new, m_new = _mlstm_sample(u, b_i[l], b_f[l], gains["g_mh"], state_C[l], state_n[l], m0)
        qcat, ckv, kpe, qabs = _mla_proj(u, gains["g_q"], gains["g_kv"], w_uq_b[l], w_uk_b[l], None,
                                         cos_s, sin_s, prompt=False)
        o_lat = _decode_attention(page_table, qabs, qcat, ckv, kpe, cache_kv_latent, cache_k_rope, l)
        ya = _latent_out(o_lat, w_uv_b[l])
        xs, h2 = _out_proj(ym, ya, xs, w_out_b[l], gains["g_post"], ga1, gains["g_pre_ff"], sc2, sh2)
        xs = _mlp(h2, xs, w_up_b[l], w_down_b[l], gains["g_post_ff"], ga2)
        for acc, val in zip(outs_s, (ckv.reshape(bs, ls, KV_RANK), kpe.reshape(bs, ls, A_ROPE),
                                     c_new, n_new, m_new[:, :, 0])):
            acc.append(val)

    return (xp.reshape(bp, lp, d), xs.reshape(bs, ls, d),
            *[jnp.stack(a) for a in outs_p], *[jnp.stack(a) for a in outs_s])
```

```python
import functools

import jax
import jax.numpy as jnp
import numpy as np
from jax import lax
from jax.experimental import pallas as pl
from jax.experimental.pallas import tpu as pltpu

F32 = jnp.float32
BF16 = jnp.bfloat16

EPS = 1e-6
M_HEADS = 4
M_DK = 128
M_DV = 256
M_CHUNK = 128
A_HEADS = 8
A_DV = 128
A_NOPE = 128
A_ROPE = 64
Q_RANK = 384
KV_RANK = 256
ROPE_THETA = 10000.0
PAST_LEN = 16384
PAGE_SIZE = 128
LANES = 128

U_WIDTH = 4096
U_Q, U_K, U_V, U_O = 0, 512, 1024, 2048
U_QLAT = 3072
U_GATES = 3456
U_KVLAT = 3584
U_KR = 3840

V7X_VMEM_LIMIT = 56 * 1024 * 1024
MLA_SCALE = float((A_NOPE + A_ROPE) ** -0.5)
MLSTM_SCALE = float(M_DK ** -0.5)
PAGES_PER_STEP = 16
STEP_BATCH = 8

_NT = (((1,), (1,)), ((), ()))


def _params(semantics, vmem=V7X_VMEM_LIMIT):
    return pltpu.CompilerParams(dimension_semantics=semantics, vmem_limit_bytes=vmem)


def _rms(x):
    return x * lax.rsqrt(jnp.mean(x * x, axis=-1, keepdims=True) + EPS)


def _log_sigmoid(x):
    return jnp.minimum(x, 0.0) - jnp.log1p(jnp.exp(-jnp.abs(x)))


def _swap_halves(x):
    return jnp.concatenate([x[:, 64:], x[:, :64]], axis=1)


def _ada_kernel(c_ref, w_ref, b_ref, o_ref):
    c = c_ref[...]
    a = (c * jax.nn.sigmoid(c)).astype(BF16)
    o_ref[...] = jnp.dot(a, w_ref[...].astype(BF16), preferred_element_type=F32) + b_ref[...]


def _ada(c_all, w_ada, b_ada):
    depth, d, n = w_ada.shape
    r = c_all.shape[0]
    tn = 1024
    return pl.pallas_call(
        _ada_kernel,
        grid=(depth, n // tn),
        in_specs=[
            pl.BlockSpec((r, d), lambda l, j: (0, 0)),
            pl.BlockSpec((None, d, tn), lambda l, j: (l, 0, j)),
            pl.BlockSpec((None, 1, tn), lambda l, j: (l, 0, j)),
        ],
        out_specs=pl.BlockSpec((None, r, tn), lambda l, j: (l, 0, j)),
        out_shape=jax.ShapeDtypeStruct((depth, r, n), F32),
        compiler_params=_params(("arbitrary", "arbitrary")),
        name="ada",
    )(c_all, w_ada, b_ada.reshape(depth, 1, n))


def _in_proj_kernel(x_ref, g_ref, sc_ref, sh_ref, w_ref, o_ref, h_ref):
    @pl.when(pl.program_id(1) == 0)
    def _():
        h = (_rms(x_ref[...]) * g_ref[...]) * (1.0 + sc_ref[...]) + sh_ref[...]
        h_ref[...] = h.astype(BF16)

    o_ref[...] = jnp.dot(h_ref[...], w_ref[...], preferred_element_type=F32)


def _mod_spec(mod, tm, t):
    g, r, d = mod.shape
    rows_per_group = t // g
    assert r in (1, tm) and rows_per_group % tm == 0
    return pl.BlockSpec((None, r, d), lambda i, *_: ((i * tm) // rows_per_group, 0, 0))


def _in_proj(x, g, sc, sh, w, layer):
    t, d = x.shape
    n = w.shape[2]
    tm = min(1024, t)
    tn = 1024
    return pl.pallas_call(
        _in_proj_kernel,
        grid=(t // tm, n // tn),
        in_specs=[
            pl.BlockSpec((tm, d), lambda i, j: (i, 0)),
            pl.BlockSpec((1, d), lambda i, j: (0, 0)),
            _mod_spec(sc, tm, t),
            _mod_spec(sh, tm, t),
            pl.BlockSpec((None, d, tn), lambda i, j: (layer, 0, j)),
        ],
        out_specs=pl.BlockSpec((tm, tn), lambda i, j: (i, j)),
        out_shape=jax.ShapeDtypeStruct((t, n), F32),
        scratch_shapes=[pltpu.VMEM((tm, d), BF16)],
        compiler_params=_params(("arbitrary", "arbitrary")),
        name="in_proj",
    )(x, g, sc, sh, w)


def _head_out(hh, gmh, o_pre):
    return ((_rms(hh) * gmh) * jax.nn.sigmoid(o_pre)).astype(BF16)


def _mlstm_chunk_kernel(bi_ref, bf_ref, q_ref, k_ref, v_ref, o_ref, g_ref, gmh_ref,
                        ym_ref, cout_ref, nout_ref, mout_ref, ct_ref, n_ref, m_ref):
    c = pl.program_id(1)
    cs = M_CHUNK

    @pl.when(c == 0)
    def _():
        ct_ref[...] = jnp.zeros_like(ct_ref)
        n_ref[...] = jnp.zeros_like(n_ref)
        m_ref[...] = jnp.zeros_like(m_ref)

    g = g_ref[...]
    gt = g.T
    row = lax.broadcasted_iota(jnp.int32, (cs, cs), 0)
    col = lax.broadcasted_iota(jnp.int32, (cs, cs), 1)
    causal = col <= row

    for h in range(M_HEADS):
        i_col = g[:, h:h + 1] + bi_ref[h]
        f_col = _log_sigmoid(g[:, M_HEADS + h:M_HEADS + h + 1] + bf_ref[h])
        i_row = gt[h:h + 1, :] + bi_ref[h]
        f_row = _log_sigmoid(gt[M_HEADS + h:M_HEADS + h + 1, :] + bf_ref[h])
        b_col = jnp.sum(jnp.where(causal, f_row, 0.0), axis=1, keepdims=True)
        b_row = jnp.sum(jnp.where(row <= col, f_col, 0.0), axis=0, keepdims=True)
        m_prev = m_ref[h][:, :1]

        dm = jnp.where(causal, b_col - b_row + i_row, -jnp.inf)
        log_inter = b_col + m_prev
        m_t = jnp.maximum(log_inter, jnp.max(dm, axis=1, keepdims=True))
        w_inter = jnp.exp(log_inter - m_t) * MLSTM_SCALE
        p = jnp.exp(dm - m_t)

        q = q_ref[:, h * M_DK:(h + 1) * M_DK]
        k = k_ref[:, h * M_DK:(h + 1) * M_DK]
        v = v_ref[:, h * M_DV:(h + 1) * M_DV]
        qb = q.astype(BF16)
        ktb = k.T.astype(BF16)
        ct = ct_ref[h]
        n = n_ref[h]

        s = jnp.dot(qb, ktb, preferred_element_type=F32) * (p * MLSTM_SCALE)
        num = w_inter * jnp.dot(qb, ct.astype(BF16), preferred_element_type=F32) \
            + jnp.dot(s.astype(BF16), v.astype(BF16), preferred_element_type=F32)
        den = w_inter * jnp.sum(q * n, axis=1, keepdims=True) + jnp.sum(s, axis=1, keepdims=True)
        hh = num / jnp.maximum(jnp.abs(den), jnp.exp(-m_t))

        m_new = m_t[cs - 1:cs, :]
        b_last = b_col[cs - 1:cs, :]
        a_inter = jnp.exp(b_last + m_prev - m_new)
        a_col = jnp.exp(b_last - b_col + i_col - m_new)
        ct_ref[h] = a_inter * ct + jnp.dot(ktb, (v * a_col).astype(BF16), preferred_element_type=F32)
        n_ref[h] = a_inter * n + jnp.sum(a_col * k, axis=0, keepdims=True)
        m_ref[h] = jnp.broadcast_to(m_new, (1, LANES))

        sl = slice(h * M_DV, (h + 1) * M_DV)
        ym_ref[:, sl] = _head_out(hh, gmh_ref[:, sl], o_ref[:, sl])

    @pl.when(c == pl.num_programs(1) - 1)
    def _():
        for h in range(M_HEADS):
            cout_ref[h] = ct_ref[h].T
            nout_ref[h:h + 1, :] = n_ref[h]
            mout_ref[h:h + 1, :] = m_ref[h]


def _mlstm_prompt(u, b_i, b_f, g_mh, batch, seq):
    t = u.shape[0]
    nc = seq // M_CHUNK
    cs = M_CHUNK
    qw, vw = M_HEADS * M_DK, M_HEADS * M_DV
    rowmap = lambda b, c: b * nc + c
    smem = pl.BlockSpec(memory_space=pltpu.SMEM)
    return pl.pallas_call(
        _mlstm_chunk_kernel,
        grid=(batch, nc),
        in_specs=[
            smem, smem,
            pl.BlockSpec((cs, qw), lambda b, c: (rowmap(b, c), U_Q // qw)),
            pl.BlockSpec((cs, qw), lambda b, c: (rowmap(b, c), U_K // qw)),
            pl.BlockSpec((cs, vw), lambda b, c: (rowmap(b, c), U_V // vw)),
            pl.BlockSpec((cs, vw), lambda b, c: (rowmap(b, c), U_O // vw)),
            pl.BlockSpec((cs, LANES), lambda b, c: (rowmap(b, c), U_GATES // LANES)),
            pl.BlockSpec((1, vw), lambda b, c: (0, 0)),
        ],
        out_specs=[
            pl.BlockSpec((cs, vw), lambda b, c: (rowmap(b, c), 0)),
            pl.BlockSpec((None, M_HEADS, M_DV, M_DK), lambda b, c: (b, 0, 0, 0)),
            pl.BlockSpec((None, M_HEADS, M_DK), lambda b, c: (b, 0, 0)),
            pl.BlockSpec((None, M_HEADS, LANES), lambda b, c: (b, 0, 0)),
        ],
        out_shape=[
            jax.ShapeDtypeStruct((t, vw), BF16),
            jax.ShapeDtypeStruct((batch, M_HEADS, M_DV, M_DK), F32),
            jax.ShapeDtypeStruct((batch, M_HEADS, M_DK), F32),
            jax.ShapeDtypeStruct((batch, M_HEADS, LANES), F32),
        ],
        scratch_shapes=[
            pltpu.VMEM((M_HEADS, M_DK, M_DV), F32),
            pltpu.VMEM((M_HEADS, 1, M_DK), F32),
            pltpu.VMEM((M_HEADS, 1, LANES), F32),
        ],
        compiler_params=_params(("arbitrary", "arbitrary")),
        name="mlstm_prompt",
    )(b_i, b_f, u, u, u, u, u, g_mh)


def _mlstm_step_kernel(bi_ref, bf_ref, q_ref, k_ref, v_ref, o_ref, g_ref, gmh_ref, c0_ref, n0_ref, m0_ref,
                       ym_ref, cout_ref, nout_ref, mout_ref):
    eye = (lax.broadcasted_iota(jnp.int32, (M_DV, M_DV), 0)
           == lax.broadcasted_iota(jnp.int32, (M_DV, M_DV), 1))
    for b, h in [(b, h) for b in range(STEP_BATCH) for h in range(M_HEADS)]:
        g = g_ref[b]
        ig = g[:, h:h + 1] + bi_ref[h]
        lf = _log_sigmoid(g[:, M_HEADS + h:M_HEADS + h + 1] + bf_ref[h])
        m0 = m0_ref[b, h:h + 1, :1]
        log_inter = lf + m0
        m_t = jnp.maximum(log_inter, ig)
        w_inter = jnp.exp(log_inter - m_t)
        p = jnp.exp(ig - m_t)

        q = q_ref[b, :, h * M_DK:(h + 1) * M_DK]
        k = k_ref[b, :, h * M_DK:(h + 1) * M_DK]
        v = v_ref[b, :, h * M_DV:(h + 1) * M_DV]
        c0 = c0_ref[b, h]
        n0 = n0_ref[b, h:h + 1, :]

        s = jnp.sum(q * k, axis=1, keepdims=True) * (p * MLSTM_SCALE)
        q8 = jnp.broadcast_to(q, (8, M_DK)).astype(BF16)
        cq = lax.dot_general(q8, c0.astype(BF16), _NT, preferred_element_type=F32)[:1, :]
        wq = w_inter * MLSTM_SCALE
        num = wq * cq + s * v
        den = wq * jnp.sum(n0 * q, axis=1, keepdims=True) + s
        hh = num / jnp.maximum(jnp.abs(den), jnp.exp(-m_t))

        v_col = jnp.sum(jnp.where(eye, jnp.broadcast_to(v, (M_DV, M_DV)), 0.0), axis=1, keepdims=True)
        cout_ref[b, h] = w_inter * c0 + (p * v_col) * k
        nout_ref[b, h:h + 1, :] = w_inter * n0 + p * k
        mout_ref[b, h:h + 1, :] = jnp.broadcast_to(m_t, (1, LANES))

        sl = slice(h * M_DV, (h + 1) * M_DV)
        ym_ref[b, :, sl] = _head_out(hh, gmh_ref[:, sl], o_ref[b, :, sl])


def _mlstm_sample(u, b_i, b_f, g_mh, c0, n0, m0, layer):
    batch = u.shape[0]
    sb = STEP_BATCH
    assert batch % sb == 0
    qw, vw = M_HEADS * M_DK, M_HEADS * M_DV
    u3 = u.reshape(batch, 1, U_WIDTH)
    smem = pl.BlockSpec(memory_space=pltpu.SMEM)
    ym, c, n, m = pl.pallas_call(
        _mlstm_step_kernel,
        grid=(batch // sb,),
        in_specs=[
            smem, smem,
            pl.BlockSpec((sb, 1, qw), lambda b: (b, 0, U_Q // qw)),
            pl.BlockSpec((sb, 1, qw), lambda b: (b, 0, U_K // qw)),
            pl.BlockSpec((sb, 1, vw), lambda b: (b, 0, U_V // vw)),
            pl.BlockSpec((sb, 1, vw), lambda b: (b, 0, U_O // vw)),
            pl.BlockSpec((sb, 1, LANES), lambda b: (b, 0, U_GATES // LANES)),
            pl.BlockSpec((1, vw), lambda b: (0, 0)),
            pl.BlockSpec((None, sb, M_HEADS, M_DV, M_DK), lambda b: (layer, b, 0, 0, 0)),
            pl.BlockSpec((None, sb, M_HEADS, M_DK), lambda b: (layer, b, 0, 0)),
            pl.BlockSpec((sb, M_HEADS, LANES), lambda b: (b, 0, 0)),
        ],
        out_specs=[
            pl.BlockSpec((sb, 1, vw), lambda b: (b, 0, 0)),
            pl.BlockSpec((sb, M_HEADS, M_DV, M_DK), lambda b: (b, 0, 0, 0)),
            pl.BlockSpec((sb, M_HEADS, M_DK), lambda b: (b, 0, 0)),
            pl.BlockSpec((sb, M_HEADS, LANES), lambda b: (b, 0, 0)),
        ],
        out_shape=[
            jax.ShapeDtypeStruct((batch, 1, vw), BF16),
            jax.ShapeDtypeStruct((batch, M_HEADS, M_DV, M_DK), F32),
            jax.ShapeDtypeStruct((batch, M_HEADS, M_DK), F32),
            jax.ShapeDtypeStruct((batch, M_HEADS, LANES), F32),
        ],
        compiler_params=_params(("arbitrary",)),
        name="mlstm_sample",
    )(b_i, b_f, u3, u3, u3, u3, u3, g_mh, c0, n0, m0)
    return ym.reshape(batch, vw), c, n, m


def _mla_common(ql_ref, kvl_ref, kr_ref, gq_ref, gkv_ref, wq_ref, cos_ref, sin_ref,
                qcat_ref, ckv_ref, kpe_ref):
    cos = cos_ref[...]
    sin = sin_ref[...]
    hq = (_rms(ql_ref[...]) * gq_ref[...]).astype(BF16)
    q = jnp.dot(hq, wq_ref[...], preferred_element_type=F32)
    for h in range(A_HEADS):
        base = h * 2 * LANES
        qr = q[:, base + LANES:base + 2 * LANES]
        qcat_ref[:, base:base + LANES] = q[:, base:base + LANES].astype(BF16)
        qcat_ref[:, base + LANES:base + 2 * LANES] = (qr * cos + _swap_halves(qr) * sin).astype(BF16)
    ckv = _rms(kvl_ref[...]) * gkv_ref[...]
    ckv_ref[...] = ckv
    kr = kr_ref[...]
    kpe128 = kr * cos + _swap_halves(kr) * sin
    kpe_ref[...] = kpe128[:, :A_ROPE]
    return q, ckv, kpe128


def _mla_proj_prompt_kernel(ql_ref, kvl_ref, kr_ref, gq_ref, gkv_ref, wq_ref, wuk_ref, wuv_ref, cos_ref, sin_ref,
                            qcat_ref, ckv_ref, kpe_ref, kn_ref, v_ref, kpe128_ref):
    _, ckv, kpe128 = _mla_common(ql_ref, kvl_ref, kr_ref, gq_ref, gkv_ref, wq_ref, cos_ref, sin_ref,
                                 qcat_ref, ckv_ref, kpe_ref)
    cb = ckv.astype(BF16)
    kn_ref[...] = jnp.dot(cb, wuk_ref[...], preferred_element_type=F32).astype(BF16)
    v_ref[...] = jnp.dot(cb, wuv_ref[...], preferred_element_type=F32).astype(BF16)
    kpe128_ref[...] = kpe128.astype(BF16)


def _mla_proj_sample_kernel(ql_ref, kvl_ref, kr_ref, gq_ref, gkv_ref, wq_ref, wuk_ref, cos_ref, sin_ref,
                            qcat_ref, ckv_ref, kpe_ref, qabs_ref):
    q, _, _ = _mla_common(ql_ref, kvl_ref, kr_ref, gq_ref, gkv_ref, wq_ref, cos_ref, sin_ref,
                          qcat_ref, ckv_ref, kpe_ref)
    for h in range(A_HEADS):
        qn = q[:, h * 2 * LANES:h * 2 * LANES + A_NOPE].astype(BF16)
        wk = wuk_ref[:, h * A_NOPE:(h + 1) * A_NOPE]
        qabs_ref[:, h * KV_RANK:(h + 1) * KV_RANK] = lax.dot_general(
            qn, wk, _NT, preferred_element_type=F32).astype(BF16)


def _mla_proj(u, g_q, g_kv, wq, wuk, wuv, cos, sin, layer, *, prompt):
    t = u.shape[0]
    tm = min(512, t)
    npos = cos.shape[0] // tm
    hw = A_HEADS * 2 * LANES
    full = lambda a: pl.BlockSpec(a.shape, lambda i: (0, 0))
    of_layer = lambda a: pl.BlockSpec((None,) + a.shape[1:], lambda i: (layer, 0, 0))
    in_specs = [
        pl.BlockSpec((tm, Q_RANK), lambda i: (i, U_QLAT // Q_RANK)),
        pl.BlockSpec((tm, KV_RANK), lambda i: (i, U_KVLAT // KV_RANK)),
        pl.BlockSpec((tm, LANES), lambda i: (i, U_KR // LANES)),
        full(g_q), full(g_kv), of_layer(wq), of_layer(wuk),
    ]
    args = [u, u, u, g_q, g_kv, wq, wuk]
    if prompt:
        in_specs.append(of_layer(wuv))
        args.append(wuv)
    in_specs += [pl.BlockSpec((tm, LANES), lambda i: (i % npos, 0))] * 2
    args += [cos, sin]
    row = lambda w: pl.BlockSpec((tm, w), lambda i: (i, 0))
    out_specs = [row(hw), row(KV_RANK), row(A_ROPE)]
    out_shape = [jax.ShapeDtypeStruct((t, hw), BF16), jax.ShapeDtypeStruct((t, KV_RANK), F32),
                 jax.ShapeDtypeStruct((t, A_ROPE), F32)]
    if prompt:
        out_specs += [row(A_HEADS * A_NOPE), row(A_HEADS * A_DV), row(LANES)]
        out_shape += [jax.ShapeDtypeStruct((t, A_HEADS * A_NOPE), BF16),
                      jax.ShapeDtypeStruct((t, A_HEADS * A_DV), BF16),
                      jax.ShapeDtypeStruct((t, LANES), BF16)]
        body = _mla_proj_prompt_kernel
    else:
        out_specs += [row(A_HEADS * KV_RANK)]
        out_shape += [jax.ShapeDtypeStruct((t, A_HEADS * KV_RANK), BF16)]
        body = _mla_proj_sample_kernel
    return pl.pallas_call(
        body,
        grid=(t // tm,),
        in_specs=in_specs,
        out_specs=out_specs,
        out_shape=out_shape,
        compiler_params=_params(("arbitrary",)),
        name="mla_proj_prompt" if prompt else "mla_proj_sample",
    )(*args)


FLASH_HEADS = 4
EXP2_SCALE = float(MLA_SCALE * np.log2(np.e))


def _flash_kernel(q_ref, kn_ref, kpe_ref, v_ref, o_ref, m_ref, acc_ref, *, tile):
    qi = pl.program_id(2)
    m_ref[...] = jnp.full_like(m_ref, -jnp.inf)
    acc_ref[...] = jnp.zeros_like(acc_ref)
    ones = jnp.ones((tile, LANES), BF16)
    reps = tile // LANES

    def step(ki, diagonal):
        off = pl.multiple_of(ki * tile, tile)
        kpe = kpe_ref[pl.ds(off, tile), :]
        if diagonal:
            row = lax.broadcasted_iota(jnp.int32, (tile, tile), 0)
            col = lax.broadcasted_iota(jnp.int32, (tile, tile), 1)
            keep = col <= row
        for h in range(FLASH_HEADS):
            q = q_ref[:, h * 2 * LANES:(h + 1) * 2 * LANES]
            k = jnp.concatenate([kn_ref[pl.ds(off, tile), h * A_NOPE:(h + 1) * A_NOPE], kpe], axis=1)
            v = jnp.concatenate([v_ref[pl.ds(off, tile), h * A_DV:(h + 1) * A_DV], ones], axis=1)
            s = lax.dot_general(q, k, _NT, preferred_element_type=F32)
            if diagonal:
                s = jnp.where(keep, s, -jnp.inf)
            m_prev = m_ref[h]
            m_new = jnp.maximum(m_prev, jnp.max(s, axis=1, keepdims=True))
            alpha = jnp.exp2((m_prev - m_new) * EXP2_SCALE)
            p = jnp.exp2((s - jnp.concatenate([m_new] * reps, axis=1)) * EXP2_SCALE)
            acc_ref[h] = jnp.concatenate([alpha, alpha], axis=1) * acc_ref[h] \
                + jnp.dot(p.astype(BF16), v, preferred_element_type=F32)
            m_ref[h] = m_new

    def body(ki, carry):
        step(ki, False)
        return carry

    lax.fori_loop(0, qi, body, 0)
    step(qi, True)
    for h in range(FLASH_HEADS):
        acc = acc_ref[h]
        o_ref[:, h * A_DV:(h + 1) * A_DV] = (acc[:, :A_DV] / acc[:, A_DV:]).astype(BF16)


def _flash(qcat, kn, kpe128, v, batch, seq):
    t = qcat.shape[0]
    tile = min(512, seq)
    nq = seq // tile
    hp = FLASH_HEADS
    return pl.pallas_call(
        functools.partial(_flash_kernel, tile=tile),
        grid=(batch, A_HEADS // hp, nq),
        in_specs=[
            pl.BlockSpec((tile, hp * 2 * LANES), lambda b, h, i: (b * nq + i, h)),
            pl.BlockSpec((seq, hp * A_NOPE), lambda b, h, i: (b, h)),
            pl.BlockSpec((seq, LANES), lambda b, h, i: (b, 0)),
            pl.BlockSpec((seq, hp * A_DV), lambda b, h, i: (b, h)),
        ],
        out_specs=pl.BlockSpec((tile, hp * A_DV), lambda b, h, i: (b * nq + i, h)),
        out_shape=jax.ShapeDtypeStruct((t, A_HEADS * A_DV), BF16),
        scratch_shapes=[pltpu.VMEM((hp, tile, LANES), F32), pltpu.VMEM((hp, tile, 2 * A_DV), F32)],
        compiler_params=_params(("arbitrary", "arbitrary", "arbitrary")),
        name="mla_flash",
    )(qcat, kn, kpe128, v)


def _decode_kernel(pt_ref, qabs_ref, qcat_ref, ckvn_ref, kpen_ref, kv_hbm, kr_hbm, o_ref,
                   kv_buf, kr_buf, kv_sem, kr_sem, *, layer, pages, n_chunks):
    b = pl.program_id(0)
    span = pages * PAGE_SIZE

    def kv_copy(page, slot, i):
        return pltpu.make_async_copy(kv_hbm.at[layer, page], kv_buf.at[slot, pl.ds(i * PAGE_SIZE, PAGE_SIZE), :],
                                     kv_sem.at[slot])

    def kr_copy(page, slot, i):
        return pltpu.make_async_copy(kr_hbm.at[layer, page], kr_buf.at[slot, :, pl.ds(i * PAGE_SIZE, PAGE_SIZE)],
                                     kr_sem.at[slot])

    def start_chunk(seq, c, slot):
        for i in range(pages):
            page = pt_ref[seq, c * pages + i]
            kv_copy(page, slot, i).start()
            kr_copy(page, slot, i).start()

    def wait_chunk(slot):
        for i in range(pages):
            kv_copy(0, slot, i).wait()
            kr_copy(0, slot, i).wait()

    @pl.when(b == 0)
    def _():
        start_chunk(0, 0, 0)

    qa = qabs_ref[...]
    qp = qcat_ref[:, A_NOPE:A_NOPE + A_ROPE]
    m = jnp.full((A_HEADS, 1), -jnp.inf, F32)
    l = jnp.zeros((A_HEADS, 1), F32)
    acc = jnp.zeros((A_HEADS, KV_RANK), F32)
    for c in range(n_chunks):
        slot = c % 2
        if c + 1 < n_chunks:
            start_chunk(b, c + 1, 1 - slot)
        else:
            @pl.when(b + 1 < pl.num_programs(0))
            def _():
                start_chunk(b + 1, 0, 1 - slot)
        wait_chunk(slot)
        kvb = kv_buf[slot].astype(BF16)
        krb = kr_buf[slot].astype(BF16)
        s = (lax.dot_general(qa, kvb, _NT, preferred_element_type=F32)
             + jnp.dot(qp, krb, preferred_element_type=F32)) * MLA_SCALE
        m_new = jnp.maximum(m, jnp.max(s, axis=1, keepdims=True))
        alpha = jnp.exp(m - m_new)
        p = jnp.exp(s - m_new)
        l = alpha * l + jnp.sum(p, axis=1, keepdims=True)
        acc = alpha * acc + jnp.dot(p.astype(BF16), kvb, preferred_element_type=F32)
        m = m_new

    ckv_n = ckvn_ref[...]
    kpe_n = kpen_ref[...]
    s_n = (jnp.sum(qa.astype(F32) * ckv_n, axis=1, keepdims=True)
           + jnp.sum(qp.astype(F32) * kpe_n, axis=1, keepdims=True)) * MLA_SCALE
    m_f = jnp.maximum(m, s_n)
    a_past = jnp.exp(m - m_f)
    a_new = jnp.exp(s_n - m_f)
    o_ref[...] = (acc * a_past + a_new * ckv_n) / (l * a_past + a_new)


def _decode_attention(page_table, qabs, qcat, ckv_new, kpe_new, cache_kv, cache_kr_t, layer):
    batch, n_pages = page_table.shape
    pages = PAGES_PER_STEP
    n_chunks = n_pages // pages
    assert n_pages % pages == 0 and n_chunks % 2 == 0
    qabs3 = qabs.reshape(batch, A_HEADS, KV_RANK)
    qcat3 = qcat.reshape(batch, A_HEADS, 2 * LANES)
    span = pages * PAGE_SIZE

    per_b = lambda shape: pl.BlockSpec((None,) + shape, lambda b, pt: (b, 0, 0))
    hbm = pl.BlockSpec(memory_space=pl.ANY)
    grid_spec = pltpu.PrefetchScalarGridSpec(
        num_scalar_prefetch=1,
        grid=(batch,),
        in_specs=[per_b((A_HEADS, KV_RANK)), per_b((A_HEADS, 2 * LANES)), per_b((1, KV_RANK)), per_b((1, A_ROPE)),
                  hbm, hbm],
        out_specs=per_b((A_HEADS, KV_RANK)),
        scratch_shapes=[pltpu.VMEM((2, span, KV_RANK), F32),
                        pltpu.VMEM((2, A_ROPE, span), F32),
                        pltpu.SemaphoreType.DMA((2,)),
                        pltpu.SemaphoreType.DMA((2,))],
    )
    o_lat = pl.pallas_call(
        functools.partial(_decode_kernel, layer=layer, pages=pages, n_chunks=n_chunks),
        grid_spec=grid_spec,
        out_shape=jax.ShapeDtypeStruct((batch, A_HEADS, KV_RANK), F32),
        compiler_params=_params(("arbitrary",)),
        name="mla_decode",
    )(page_table, qabs3, qcat3, ckv_new.reshape(batch, 1, KV_RANK), kpe_new.reshape(batch, 1, A_ROPE),
      cache_kv, cache_kr_t)
    return o_lat.reshape(batch, A_HEADS * KV_RANK)


def _latent_out_kernel(o_ref, wuv_ref, ya_ref):
    for h in range(A_HEADS):
        ya_ref[:, h * A_DV:(h + 1) * A_DV] = jnp.dot(
            o_ref[:, h * KV_RANK:(h + 1) * KV_RANK].astype(BF16), wuv_ref[:, h * A_DV:(h + 1) * A_DV],
            preferred_element_type=F32).astype(BF16)


def _latent_out(o_lat, wuv, layer):
    batch = o_lat.shape[0]
    return pl.pallas_call(
        _latent_out_kernel,
        grid=(1,),
        in_specs=[pl.BlockSpec(o_lat.shape, lambda i: (0, 0)),
                  pl.BlockSpec((None,) + wuv.shape[1:], lambda i: (layer, 0, 0))],
        out_specs=pl.BlockSpec((batch, A_HEADS * A_DV), lambda i: (0, 0)),
        out_shape=jax.ShapeDtypeStruct((batch, A_HEADS * A_DV), BF16),
        compiler_params=_params(("arbitrary",)),
        name="mla_latent_out",
    )(o_lat, wuv)


def _out_proj_kernel(ym_ref, ya_ref, x_ref, w1_ref, w2_ref, gpost_ref, ga_ref, gpre_ref, sc_ref, sh_ref,
                     xo_ref, h_ref):
    y = jnp.dot(ym_ref[...], w1_ref[...], preferred_element_type=F32) \
        + jnp.dot(ya_ref[...], w2_ref[...], preferred_element_type=F32)
    xn = x_ref[...] + ga_ref[...] * (_rms(y) * gpost_ref[...])
    xo_ref[...] = xn
    h_ref[...] = ((_rms(xn) * gpre_ref[...]) * (1.0 + sc_ref[...]) + sh_ref[...]).astype(BF16)


def _out_proj(ym, ya, x, w_out, g_post, ga, g_pre, sc, sh, layer):
    t, d = x.shape
    half = ym.shape[1]
    tm = min(256, t)
    vec = pl.BlockSpec((1, d), lambda i: (0, 0))
    return pl.pallas_call(
        _out_proj_kernel,
        grid=(t // tm,),
        in_specs=[
            pl.BlockSpec((tm, half), lambda i: (i, 0)),
            pl.BlockSpec((tm, half), lambda i: (i, 0)),
            pl.BlockSpec((tm, d), lambda i: (i, 0)),
            pl.BlockSpec((None, half, d), lambda i: (layer, 0, 0)),
            pl.BlockSpec((None, half, d), lambda i: (layer, 1, 0)),
            vec, _mod_spec(ga, tm, t), vec, _mod_spec(sc, tm, t), _mod_spec(sh, tm, t),
        ],
        out_specs=[pl.BlockSpec((tm, d), lambda i: (i, 0)), pl.BlockSpec((tm, d), lambda i: (i, 0))],
        out_shape=[jax.ShapeDtypeStruct((t, d), F32), jax.ShapeDtypeStruct((t, d), BF16)],
        compiler_params=_params(("arbitrary",)),
        name="out_proj",
    )(ym, ya, x, w_out, w_out, g_post, ga, g_pre, sc, sh)


def _mlp_kernel(h_ref, wu_ref, wd_ref, x_ref, gpost_ref, ga_ref, xo_ref, acc_ref):
    f = pl.program_id(1)

    @pl.when(f == 0)
    def _():
        acc_ref[...] = jnp.zeros_like(acc_ref)

    a = jnp.dot(h_ref[...], wu_ref[...], preferred_element_type=F32)
    a = jnp.square(jnp.maximum(a, 0.0)).astype(BF16)
    acc_ref[...] += jnp.dot(a, wd_ref[...], preferred_element_type=F32)

    @pl.when(f == pl.num_programs(1) - 1)
    def _():
        xo_ref[...] = x_ref[...] + ga_ref[...] * (_rms(acc_ref[...]) * gpost_ref[...])


def _mlp(h, x, w_up, w_down, g_post, ga, layer):
    t, d = x.shape
    ff = w_up.shape[2]
    tm = min(512, t)
    tf = 1024
    return pl.pallas_call(
        _mlp_kernel,
        grid=(t // tm, ff // tf),
        in_specs=[
            pl.BlockSpec((tm, d), lambda i, f: (i, 0)),
            pl.BlockSpec((None, d, tf), lambda i, f: (layer, 0, f)),
            pl.BlockSpec((None, tf, d), lambda i, f: (layer, f, 0)),
            pl.BlockSpec((tm, d), lambda i, f: (i, 0)),
            pl.BlockSpec((1, d), lambda i, f: (0, 0)),
            _mod_spec(ga, tm, t),
        ],
        out_specs=pl.BlockSpec((tm, d), lambda i, f: (i, 0)),
        out_shape=jax.ShapeDtypeStruct((t, d), F32),
        scratch_shapes=[pltpu.VMEM((tm, d), F32)],
        compiler_params=_params(("arbitrary", "arbitrary")),
        name="mlp",
    )(h, w_up, w_down, x, g_post, ga)


def _prep_w_in(w_in):
    qw, vw = M_HEADS * M_DK, M_HEADS * M_DV
    o = 2 * qw + 2 * vw
    gates = w_in[..., o:o + 2 * M_HEADS]
    o += 2 * M_HEADS
    q_lat = w_in[..., o:o + Q_RANK]
    o += Q_RANK
    kv_lat = w_in[..., o:o + KV_RANK]
    o += KV_RANK
    k_r = w_in[..., o:o + A_ROPE]
    half = A_ROPE // 2
    zeros = lambda n: jnp.zeros(w_in.shape[:-1] + (n,), w_in.dtype)
    out = jnp.concatenate([
        w_in[..., :2 * qw + 2 * vw],
        q_lat,
        gates, zeros(LANES - 2 * M_HEADS),
        kv_lat,
        k_r, k_r[..., half:], k_r[..., :half],
        zeros(U_WIDTH - U_KR - LANES),
    ], axis=-1)
    assert out.shape[-1] == U_WIDTH
    return out.astype(BF16)


def _prep_w_uq(w_uq):
    depth, r, _ = w_uq.shape
    w = w_uq.reshape(depth, r, A_HEADS, A_NOPE + A_ROPE)
    half = A_ROPE // 2
    x1 = w[..., A_NOPE:A_NOPE + half]
    x2 = w[..., A_NOPE + half:]
    out = jnp.concatenate([w[..., :A_NOPE], x1, x2, x2, x1], axis=-1)
    return out.reshape(depth, r, A_HEADS * 2 * LANES).astype(BF16)


def _rope_tables(pos):
    freqs = ROPE_THETA ** (-jnp.arange(0, A_ROPE, 2, dtype=F32) / A_ROPE)
    ang = pos[:, None] * freqs[None, :]
    cos, sin = jnp.cos(ang), jnp.sin(ang)
    z = jnp.zeros((pos.shape[0], A_ROPE), F32)
    return jnp.concatenate([cos, cos, z], axis=1), jnp.concatenate([-sin, sin, z], axis=1)


def _split_mod(ada, groups, rows):
    d = ada.shape[1] // 6
    return [ada[:, i * d:(i + 1) * d].reshape(groups, rows, d) for i in range(6)]


def kernel(x_prompt, x_sample, c_prompt, c_sample, cache_kv_latent, cache_k_rope, state_C, state_n, state_m,
           page_table, w_ada, b_ada, g_pre_mix, g_post_mix, g_pre_ff, g_post_ff, w_in, b_i, b_f, g_mh,
           g_q, w_uq, g_kv, w_uk, w_uv, w_out, w_up, w_down):
    bp, lp, d = x_prompt.shape
    bs, ls, _ = x_sample.shape
    assert ls == 1
    depth = w_in.shape[0]
    tp = bp * lp

    w_in_b = _prep_w_in(w_in)
    w_uq_b = _prep_w_uq(w_uq)
    w_uk_b = w_uk.astype(BF16)
    w_uv_b = w_uv.astype(BF16)
    w_out_b = w_out.astype(BF16)
    w_up_b = w_up.astype(BF16)
    w_down_b = w_down.astype(BF16)

    pad = (-(bp + bs)) % 8
    c_all = jnp.concatenate([c_prompt, c_sample, jnp.zeros((pad, d), F32)], axis=0)
    ada = _ada(c_all, w_ada, b_ada)

    cos_p, sin_p = _rope_tables(jnp.arange(lp, dtype=F32))
    cos_s, sin_s = _rope_tables(jnp.full((bs,), PAST_LEN, F32) + jnp.arange(ls, dtype=F32))

    cache_kr_t = jnp.swapaxes(cache_k_rope, 2, 3)

    xp = x_prompt.reshape(tp, d)
    xs = x_sample.reshape(bs, d)
    outs_p = [[] for _ in range(5)]
    outs_s = [[] for _ in range(5)]
    vec = lambda a: a.reshape(1, -1)

    for l in range(depth):
        gains = dict(g_pre=vec(g_pre_mix[l]), g_post=vec(g_post_mix[l]), g_pre_ff=vec(g_pre_ff[l]),
                     g_post_ff=vec(g_post_ff[l]), g_mh=vec(g_mh[l]), g_q=vec(g_q[l]), g_kv=vec(g_kv[l]))

        sh1, sc1, ga1, sh2, sc2, ga2 = _split_mod(ada[l, :bp], bp, 1)
        u = _in_proj(xp, gains["g_pre"], sc1, sh1, w_in_b, l)
        ym, c_new, n_new, m_new = _mlstm_prompt(u, b_i[l], b_f[l], gains["g_mh"], bp, lp)
        qcat, ckv, kpe, kn, v, kpe128 = _mla_proj(u, gains["g_q"], gains["g_kv"], w_uq_b, w_uk_b, w_uv_b,
                                                  cos_p, sin_p, l, prompt=True)
        ya = _flash(qcat, kn, kpe128, v, bp, lp)
        xp, h2 = _out_proj(ym, ya, xp, w_out_b, gains["g_post"], ga1, gains["g_pre_ff"], sc2, sh2, l)
        xp = _mlp(h2, xp, w_up_b, w_down_b, gains["g_post_ff"], ga2, l)
        for acc, val in zip(outs_p, (ckv.reshape(bp, lp, KV_RANK), kpe.reshape(bp, lp, A_ROPE),
                                     c_new, n_new, m_new[:, :, 0])):
            acc.append(val)

        sh1, sc1, ga1, sh2, sc2, ga2 = _split_mod(ada[l, bp:bp + bs], 1, bs)
        u = _in_proj(xs, gains["g_pre"], sc1, sh1, w_in_b, l)
        m0 = jnp.broadcast_to(state_m[l][:, :, None], (bs, M_HEADS, LANES))
        ym, c_new, n_new, m_new = _mlstm_sample(u, b_i[l], b_f[l], gains["g_mh"], state_C, state_n, m0, l)
        qcat, ckv, kpe, qabs = _mla_proj(u, gains["g_q"], gains["g_kv"], w_uq_b, w_uk_b, None,
                                         cos_s, sin_s, l, prompt=False)
        o_lat = _decode_attention(page_table, qabs, qcat, ckv, kpe, cache_kv_latent, cache_kr_t, l)
        ya = _latent_out(o_lat, w_uv_b, l)
        xs, h2 = _out_proj(ym, ya, xs, w_out_b, gains["g_post"], ga1, gains["g_pre_ff"], sc2, sh2, l)
        xs = _mlp(h2, xs, w_up_b, w_down_b, gains["g_post_ff"], ga2, l)
        for acc, val in zip(outs_s, (ckv.reshape(bs, ls, KV_RANK), kpe.reshape(bs, ls, A_ROPE),
                                     c_new, n_new, m_new[:, :, 0])):
            acc.append(val)

    return (xp.reshape(bp, lp, d), xs.reshape(bs, ls, d),
            *[jnp.stack(a) for a in outs_p], *[jnp.stack(a) for a in outs_s])
```

```python
import functools

import jax
import jax.numpy as jnp
import numpy as np
from jax import lax
from jax.experimental import pallas as pl
from jax.experimental.pallas import tpu as pltpu

F32 = jnp.float32
BF16 = jnp.bfloat16

EPS = 1e-6
M_HEADS = 4
M_DK = 128
M_DV = 256
M_CHUNK = 128
A_HEADS = 8
A_DV = 128
A_NOPE = 128
A_ROPE = 64
Q_RANK = 384
KV_RANK = 256
ROPE_THETA = 10000.0
PAST_LEN = 16384
PAGE_SIZE = 128
LANES = 128

U_WIDTH = 4096
U_Q, U_K, U_V, U_O = 0, 512, 1024, 2048
U_QLAT = 3072
U_GATES = 3456
U_KVLAT = 3584
U_KR = 3840

V7X_VMEM_LIMIT = 56 * 1024 * 1024
MLA_SCALE = float((A_NOPE + A_ROPE) ** -0.5)
MLSTM_SCALE = float(M_DK ** -0.5)
PAGES_PER_STEP = 16
DECODE_SLOTS = 4
DECODE_AHEAD = 3
STEP_BATCH = 8
CHUNK_SEQS = 4

_NT = (((1,), (1,)), ((), ()))


def _params(semantics, vmem=V7X_VMEM_LIMIT):
    return pltpu.CompilerParams(dimension_semantics=semantics, vmem_limit_bytes=vmem)


def _rms(x):
    return x * lax.rsqrt(jnp.mean(x * x, axis=-1, keepdims=True) + EPS)


def _log_sigmoid(x):
    return jnp.minimum(x, 0.0) - jnp.log1p(jnp.exp(-jnp.abs(x)))


def _swap_halves(x):
    return jnp.concatenate([x[:, 64:], x[:, :64]], axis=1)


def _ada_kernel(c_ref, w_ref, b_ref, o_ref):
    c = c_ref[...]
    a = (c * jax.nn.sigmoid(c)).astype(BF16)
    o_ref[...] = jnp.dot(a, w_ref[...].astype(BF16), preferred_element_type=F32) + b_ref[...]


def _ada(c_all, w_ada, b_ada):
    depth, d, n = w_ada.shape
    r = c_all.shape[0]
    tn = 1024
    return pl.pallas_call(
        _ada_kernel,
        grid=(depth, n // tn),
        in_specs=[
            pl.BlockSpec((r, d), lambda l, j: (0, 0)),
            pl.BlockSpec((None, d, tn), lambda l, j: (l, 0, j)),
            pl.BlockSpec((None, 1, tn), lambda l, j: (l, 0, j)),
        ],
        out_specs=pl.BlockSpec((None, r, tn), lambda l, j: (l, 0, j)),
        out_shape=jax.ShapeDtypeStruct((depth, r, n), F32),
        compiler_params=_params(("arbitrary", "arbitrary")),
        name="ada",
    )(c_all, w_ada, b_ada.reshape(depth, 1, n))


def _in_proj_kernel(x_ref, g_ref, sc_ref, sh_ref, w_ref, o_ref, h_ref):
    @pl.when(pl.program_id(1) == 0)
    def _():
        h = (_rms(x_ref[...]) * g_ref[...]) * (1.0 + sc_ref[...]) + sh_ref[...]
        h_ref[...] = h.astype(BF16)

    o_ref[...] = jnp.dot(h_ref[...], w_ref[...], preferred_element_type=F32)


def _mod_spec(mod, tm, t):
    g, r, d = mod.shape
    rows_per_group = t // g
    assert r in (1, tm) and rows_per_group % tm == 0
    return pl.BlockSpec((None, r, d), lambda i, *_: ((i * tm) // rows_per_group, 0, 0))


def _in_proj(x, g, sc, sh, w, layer):
    t, d = x.shape
    n = w.shape[2]
    tm = min(1024, t)
    tn = 1024
    return pl.pallas_call(
        _in_proj_kernel,
        grid=(t // tm, n // tn),
        in_specs=[
            pl.BlockSpec((tm, d), lambda i, j: (i, 0)),
            pl.BlockSpec((1, d), lambda i, j: (0, 0)),
            _mod_spec(sc, tm, t),
            _mod_spec(sh, tm, t),
            pl.BlockSpec((None, d, tn), lambda i, j: (layer, 0, j)),
        ],
        out_specs=pl.BlockSpec((tm, tn), lambda i, j: (i, j)),
        out_shape=jax.ShapeDtypeStruct((t, n), F32),
        scratch_shapes=[pltpu.VMEM((tm, d), BF16)],
        compiler_params=_params(("arbitrary", "arbitrary")),
        name="in_proj",
    )(x, g, sc, sh, w)


def _head_out(hh, gmh, o_pre):
    return ((_rms(hh) * gmh) * jax.nn.sigmoid(o_pre)).astype(BF16)


def _mlstm_chunk_kernel(bi_ref, bf_ref, q_ref, k_ref, v_ref, o_ref, g_ref, gmh_ref,
                        ym_ref, cout_ref, nout_ref, mout_ref, ct_ref, n_ref, m_ref, *, seqs):
    c = pl.program_id(1)
    cs = M_CHUNK

    @pl.when(c == 0)
    def _():
        ct_ref[...] = jnp.zeros_like(ct_ref)
        n_ref[...] = jnp.zeros_like(n_ref)
        m_ref[...] = jnp.zeros_like(m_ref)

    lane = lax.broadcasted_iota(jnp.int32, (1, LANES), 1)
    bias = jnp.zeros((1, LANES), F32)
    for h in range(M_HEADS):
        bias = jnp.where(lane == h, bi_ref[h], bias)
        bias = jnp.where(lane == M_HEADS + h, bf_ref[h], bias)
    row = lax.broadcasted_iota(jnp.int32, (cs, cs), 0)
    col = lax.broadcasted_iota(jnp.int32, (cs, cs), 1)
    causal = col <= row

    for sq in range(seqs):
        g = g_ref[sq] + bias
        gt = g.T[:2 * M_HEADS, :]
        ls = _log_sigmoid(g)
        lst = _log_sigmoid(gt)
        for h in range(M_HEADS):
            st = sq * M_HEADS + h
            i_col = g[:, h:h + 1]
            f_col = ls[:, M_HEADS + h:M_HEADS + h + 1]
            i_row = gt[h:h + 1, :]
            f_row = lst[M_HEADS + h:M_HEADS + h + 1, :]
            b_col = jnp.sum(jnp.where(causal, f_row, 0.0), axis=1, keepdims=True)
            b_row = jnp.sum(jnp.where(row <= col, f_col, 0.0), axis=0, keepdims=True)
            m_prev = m_ref[st][:, :1]

            dm = jnp.where(causal, b_col - b_row + i_row, -jnp.inf)
            log_inter = b_col + m_prev
            m_t = jnp.maximum(log_inter, jnp.max(dm, axis=1, keepdims=True))
            w_inter = jnp.exp(log_inter - m_t) * MLSTM_SCALE
            p = jnp.exp(dm - m_t)

            q = q_ref[sq, :, h * M_DK:(h + 1) * M_DK]
            k = k_ref[sq, :, h * M_DK:(h + 1) * M_DK]
            v = v_ref[sq, :, h * M_DV:(h + 1) * M_DV]
            qb = q.astype(BF16)
            ktb = k.T.astype(BF16)
            ct = ct_ref[st]
            n = n_ref[st]

            s = jnp.dot(qb, ktb, preferred_element_type=F32) * (p * MLSTM_SCALE)
            num = w_inter * jnp.dot(qb, ct.astype(BF16), preferred_element_type=F32) \
                + jnp.dot(s.astype(BF16), v.astype(BF16), preferred_element_type=F32)
            den = w_inter * jnp.sum(q * n, axis=1, keepdims=True) + jnp.sum(s, axis=1, keepdims=True)
            hh = num / jnp.maximum(jnp.abs(den), jnp.exp(-m_t))

            m_new = m_t[cs - 1:cs, :]
            b_last = b_col[cs - 1:cs, :]
            a_inter = jnp.exp(b_last + m_prev - m_new)
            a_col = jnp.exp(b_last - b_col + i_col - m_new)
            ct_ref[st] = a_inter * ct + jnp.dot(ktb, (v * a_col).astype(BF16), preferred_element_type=F32)
            n_ref[st] = a_inter * n + jnp.sum(a_col * k, axis=0, keepdims=True)
            m_ref[st] = jnp.broadcast_to(m_new, (1, LANES))

            sl = slice(h * M_DV, (h + 1) * M_DV)
            ym_ref[sq, :, sl] = _head_out(hh, gmh_ref[:, sl], o_ref[sq, :, sl])

    @pl.when(c == pl.num_programs(1) - 1)
    def _():
        for sq in range(seqs):
            for h in range(M_HEADS):
                st = sq * M_HEADS + h
                cout_ref[sq, h] = ct_ref[st].T
                nout_ref[sq, h:h + 1, :] = n_ref[st]
                mout_ref[sq, h:h + 1, :] = m_ref[st]


def _mlstm_prompt(u, b_i, b_f, g_mh, batch, seq):
    t = u.shape[0]
    nc = seq // M_CHUNK
    cs = M_CHUNK
    sb = min(CHUNK_SEQS, batch)
    assert batch % sb == 0
    qw, vw = M_HEADS * M_DK, M_HEADS * M_DV
    u3 = u.reshape(batch, seq, U_WIDTH)
    smem = pl.BlockSpec(memory_space=pltpu.SMEM)
    ym, c_new, n_new, m_new = pl.pallas_call(
        functools.partial(_mlstm_chunk_kernel, seqs=sb),
        grid=(batch // sb, nc),
        in_specs=[
            smem, smem,
            pl.BlockSpec((sb, cs, qw), lambda b, c: (b, c, U_Q // qw)),
            pl.BlockSpec((sb, cs, qw), lambda b, c: (b, c, U_K // qw)),
            pl.BlockSpec((sb, cs, vw), lambda b, c: (b, c, U_V // vw)),
            pl.BlockSpec((sb, cs, vw), lambda b, c: (b, c, U_O // vw)),
            pl.BlockSpec((sb, cs, LANES), lambda b, c: (b, c, U_GATES // LANES)),
            pl.BlockSpec((1, vw), lambda b, c: (0, 0)),
        ],
        out_specs=[
            pl.BlockSpec((sb, cs, vw), lambda b, c: (b, c, 0)),
            pl.BlockSpec((sb, M_HEADS, M_DV, M_DK), lambda b, c: (b, 0, 0, 0)),
            pl.BlockSpec((sb, M_HEADS, M_DK), lambda b, c: (b, 0, 0)),
            pl.BlockSpec((sb, M_HEADS, LANES), lambda b, c: (b, 0, 0)),
        ],
        out_shape=[
            jax.ShapeDtypeStruct((batch, seq, vw), BF16),
            jax.ShapeDtypeStruct((batch, M_HEADS, M_DV, M_DK), F32),
            jax.ShapeDtypeStruct((batch, M_HEADS, M_DK), F32),
            jax.ShapeDtypeStruct((batch, M_HEADS, LANES), F32),
        ],
        scratch_shapes=[
            pltpu.VMEM((sb * M_HEADS, M_DK, M_DV), F32),
            pltpu.VMEM((sb * M_HEADS, 1, M_DK), F32),
            pltpu.VMEM((sb * M_HEADS, 1, LANES), F32),
        ],
        compiler_params=_params(("arbitrary", "arbitrary")),
        name="mlstm_prompt",
    )(b_i, b_f, u3, u3, u3, u3, u3, g_mh)
    return ym.reshape(t, vw), c_new, n_new, m_new


def _mlstm_step_kernel(bi_ref, bf_ref, q_ref, k_ref, v_ref, o_ref, g_ref, gmh_ref, c0_ref, n0_ref, m0_ref,
                       ym_ref, cout_ref, nout_ref, mout_ref):
    eye = jnp.where(lax.broadcasted_iota(jnp.int32, (M_DV, M_DV), 0)
                    == lax.broadcasted_iota(jnp.int32, (M_DV, M_DV), 1), 1.0, 0.0).astype(BF16)
    for b, h in [(b, h) for b in range(STEP_BATCH) for h in range(M_HEADS)]:
        g = g_ref[b]
        ig = g[:, h:h + 1] + bi_ref[h]
        lf = _log_sigmoid(g[:, M_HEADS + h:M_HEADS + h + 1] + bf_ref[h])
        m0 = m0_ref[b, h:h + 1, :1]
        log_inter = lf + m0
        m_t = jnp.maximum(log_inter, ig)
        w_inter = jnp.exp(log_inter - m_t)
        p = jnp.exp(ig - m_t)

        q = q_ref[b, :, h * M_DK:(h + 1) * M_DK]
        k = k_ref[b, :, h * M_DK:(h + 1) * M_DK]
        v = v_ref[b, :, h * M_DV:(h + 1) * M_DV]
        c0 = c0_ref[b, h]
        n0 = n0_ref[b, h:h + 1, :]

        s = jnp.sum(q * k, axis=1, keepdims=True) * (p * MLSTM_SCALE)
        q8 = jnp.broadcast_to(q, (8, M_DK)).astype(BF16)
        cq = lax.dot_general(q8, c0.astype(BF16), _NT, preferred_element_type=F32)[:1, :]
        wq = w_inter * MLSTM_SCALE
        num = wq * cq + s * v
        den = wq * jnp.sum(n0 * q, axis=1, keepdims=True) + s
        hh = num / jnp.maximum(jnp.abs(den), jnp.exp(-m_t))

        pv = jnp.broadcast_to(p * v, (LANES, M_DV)).astype(BF16)
        pv_col = lax.dot_general(eye, pv, _NT, preferred_element_type=F32)
        cout_ref[b, h] = w_inter * c0 + pv_col * k
        nout_ref[b, h:h + 1, :] = w_inter * n0 + p * k
        mout_ref[b, h:h + 1, :] = jnp.broadcast_to(m_t, (1, LANES))

        sl = slice(h * M_DV, (h + 1) * M_DV)
        ym_ref[b, :, sl] = _head_out(hh, gmh_ref[:, sl], o_ref[b, :, sl])


def _mlstm_sample(u, b_i, b_f, g_mh, c0, n0, m0, layer):
    batch = u.shape[0]
    sb = STEP_BATCH
    assert batch % sb == 0
    qw, vw = M_HEADS * M_DK, M_HEADS * M_DV
    u3 = u.reshape(batch, 1, U_WIDTH)
    smem = pl.BlockSpec(memory_space=pltpu.SMEM)
    ym, c, n, m = pl.pallas_call(
        _mlstm_step_kernel,
        grid=(batch // sb,),
        in_specs=[
            smem, smem,
            pl.BlockSpec((sb, 1, qw), lambda b: (b, 0, U_Q // qw)),
            pl.BlockSpec((sb, 1, qw), lambda b: (b, 0, U_K // qw)),
            pl.BlockSpec((sb, 1, vw), lambda b: (b, 0, U_V // vw)),
            pl.BlockSpec((sb, 1, vw), lambda b: (b, 0, U_O // vw)),
            pl.BlockSpec((sb, 1, LANES), lambda b: (b, 0, U_GATES // LANES)),
            pl.BlockSpec((1, vw), lambda b: (0, 0)),
            pl.BlockSpec((None, sb, M_HEADS, M_DV, M_DK), lambda b: (layer, b, 0, 0, 0)),
            pl.BlockSpec((None, sb, M_HEADS, M_DK), lambda b: (layer, b, 0, 0)),
            pl.BlockSpec((sb, M_HEADS, LANES), lambda b: (b, 0, 0)),
        ],
        out_specs=[
            pl.BlockSpec((sb, 1, vw), lambda b: (b, 0, 0)),
            pl.BlockSpec((sb, M_HEADS, M_DV, M_DK), lambda b: (b, 0, 0, 0)),
            pl.BlockSpec((sb, M_HEADS, M_DK), lambda b: (b, 0, 0)),
            pl.BlockSpec((sb, M_HEADS, LANES), lambda b: (b, 0, 0)),
        ],
        out_shape=[
            jax.ShapeDtypeStruct((batch, 1, vw), BF16),
            jax.ShapeDtypeStruct((batch, M_HEADS, M_DV, M_DK), F32),
            jax.ShapeDtypeStruct((batch, M_HEADS, M_DK), F32),
            jax.ShapeDtypeStruct((batch, M_HEADS, LANES), F32),
        ],
        compiler_params=_params(("arbitrary",)),
        name="mlstm_sample",
    )(b_i, b_f, u3, u3, u3, u3, u3, g_mh, c0, n0, m0)
    return ym.reshape(batch, vw), c, n, m


def _mla_common(ql_ref, kvl_ref, kr_ref, gq_ref, gkv_ref, wq_ref, cos_ref, sin_ref,
                qcat_ref, ckv_ref, kpe_ref):
    cos = cos_ref[...]
    sin = sin_ref[...]
    hq = (_rms(ql_ref[...]) * gq_ref[...]).astype(BF16)
    q = jnp.dot(hq, wq_ref[...], preferred_element_type=F32)
    for h in range(A_HEADS):
        base = h * 2 * LANES
        qr = q[:, base + LANES:base + 2 * LANES]
        qcat_ref[:, base:base + LANES] = q[:, base:base + LANES].astype(BF16)
        qcat_ref[:, base + LANES:base + 2 * LANES] = (qr * cos + _swap_halves(qr) * sin).astype(BF16)
    ckv = _rms(kvl_ref[...]) * gkv_ref[...]
    ckv_ref[...] = ckv
    kr = kr_ref[...]
    kpe128 = kr * cos + _swap_halves(kr) * sin
    kpe_ref[...] = kpe128[:, :A_ROPE]
    return q, ckv, kpe128


def _mla_proj_prompt_kernel(ql_ref, kvl_ref, kr_ref, gq_ref, gkv_ref, wq_ref, wuk_ref, wuv_ref, cos_ref, sin_ref,
                            qcat_ref, ckv_ref, kpe_ref, kn_ref, v_ref, kpe128_ref):
    _, ckv, kpe128 = _mla_common(ql_ref, kvl_ref, kr_ref, gq_ref, gkv_ref, wq_ref, cos_ref, sin_ref,
                                 qcat_ref, ckv_ref, kpe_ref)
    cb = ckv.astype(BF16)
    kn_ref[...] = jnp.dot(cb, wuk_ref[...], preferred_element_type=F32).astype(BF16)
    v_ref[...] = jnp.dot(cb, wuv_ref[...], preferred_element_type=F32).astype(BF16)
    kpe128_ref[...] = kpe128.astype(BF16)


def _mla_proj_sample_kernel(ql_ref, kvl_ref, kr_ref, gq_ref, gkv_ref, wq_ref, wuk_ref, cos_ref, sin_ref,
                            qcat_ref, ckv_ref, kpe_ref, qabs_ref):
    q, _, _ = _mla_common(ql_ref, kvl_ref, kr_ref, gq_ref, gkv_ref, wq_ref, cos_ref, sin_ref,
                          qcat_ref, ckv_ref, kpe_ref)
    for h in range(A_HEADS):
        qn = q[:, h * 2 * LANES:h * 2 * LANES + A_NOPE].astype(BF16)
        wk = wuk_ref[:, h * A_NOPE:(h + 1) * A_NOPE]
        qabs_ref[:, h * KV_RANK:(h + 1) * KV_RANK] = lax.dot_general(
            qn, wk, _NT, preferred_element_type=F32).astype(BF16)


def _mla_proj(u, g_q, g_kv, wq, wuk, wuv, cos, sin, layer, *, prompt):
    t = u.shape[0]
    tm = min(512, t)
    npos = cos.shape[0] // tm
    hw = A_HEADS * 2 * LANES
    full = lambda a: pl.BlockSpec(a.shape, lambda i: (0, 0))
    of_layer = lambda a: pl.BlockSpec((None,) + a.shape[1:], lambda i: (layer, 0, 0))
    in_specs = [
        pl.BlockSpec((tm, Q_RANK), lambda i: (i, U_QLAT // Q_RANK)),
        pl.BlockSpec((tm, KV_RANK), lambda i: (i, U_KVLAT // KV_RANK)),
        pl.BlockSpec((tm, LANES), lambda i: (i, U_KR // LANES)),
        full(g_q), full(g_kv), of_layer(wq), of_layer(wuk),
    ]
    args = [u, u, u, g_q, g_kv, wq, wuk]
    if prompt:
        in_specs.append(of_layer(wuv))
        args.append(wuv)
    in_specs += [pl.BlockSpec((tm, LANES), lambda i: (i % npos, 0))] * 2
    args += [cos, sin]
    row = lambda w: pl.BlockSpec((tm, w), lambda i: (i, 0))
    out_specs = [row(hw), row(KV_RANK), row(A_ROPE)]
    out_shape = [jax.ShapeDtypeStruct((t, hw), BF16), jax.ShapeDtypeStruct((t, KV_RANK), F32),
                 jax.ShapeDtypeStruct((t, A_ROPE), F32)]
    if prompt:
        out_specs += [row(A_HEADS * A_NOPE), row(A_HEADS * A_DV), row(LANES)]
        out_shape += [jax.ShapeDtypeStruct((t, A_HEADS * A_NOPE), BF16),
                      jax.ShapeDtypeStruct((t, A_HEADS * A_DV), BF16),
                      jax.ShapeDtypeStruct((t, LANES), BF16)]
        body = _mla_proj_prompt_kernel
    else:
        out_specs += [row(A_HEADS * KV_RANK)]
        out_shape += [jax.ShapeDtypeStruct((t, A_HEADS * KV_RANK), BF16)]
        body = _mla_proj_sample_kernel
    return pl.pallas_call(
        body,
        grid=(t // tm,),
        in_specs=in_specs,
        out_specs=out_specs,
        out_shape=out_shape,
        compiler_params=_params(("arbitrary",)),
        name="mla_proj_prompt" if prompt else "mla_proj_sample",
    )(*args)


FLASH_HEADS = 8
EXP2_SCALE = float(MLA_SCALE * np.log2(np.e))


def _flash_kernel(q_ref, kn_ref, kpe_ref, v_ref, o_ref, m_ref, acc_ref, *, tile):
    qi = pl.program_id(2)
    m_ref[...] = jnp.full_like(m_ref, -jnp.inf)
    acc_ref[...] = jnp.zeros_like(acc_ref)
    ones = jnp.ones((tile, LANES), BF16)
    reps = tile // LANES

    def step(ki, diagonal):
        off = pl.multiple_of(ki * tile, tile)
        kpe = kpe_ref[pl.ds(off, tile), :]
        if diagonal:
            row = lax.broadcasted_iota(jnp.int32, (tile, tile), 0)
            col = lax.broadcasted_iota(jnp.int32, (tile, tile), 1)
            keep = col <= row
        for h in range(FLASH_HEADS):
            q = q_ref[:, h * 2 * LANES:(h + 1) * 2 * LANES]
            k = jnp.concatenate([kn_ref[pl.ds(off, tile), h * A_NOPE:(h + 1) * A_NOPE], kpe], axis=1)
            v = jnp.concatenate([v_ref[pl.ds(off, tile), h * A_DV:(h + 1) * A_DV], ones], axis=1)
            s = lax.dot_general(q, k, _NT, preferred_element_type=F32)
            if diagonal:
                s = jnp.where(keep, s, -jnp.inf)
            m_prev = m_ref[h]
            m_new = jnp.maximum(m_prev, jnp.max(s, axis=1, keepdims=True))
            alpha = jnp.exp2((m_prev - m_new) * EXP2_SCALE)
            p = jnp.exp2((s - jnp.concatenate([m_new] * reps, axis=1)) * EXP2_SCALE)
            acc_ref[h] = jnp.concatenate([alpha, alpha], axis=1) * acc_ref[h] \
                + jnp.dot(p.astype(BF16), v, preferred_element_type=F32)
            m_ref[h] = m_new

    def body(ki, carry):
        step(ki, False)
        return carry

    lax.fori_loop(0, qi, body, 0)
    step(qi, True)
    for h in range(FLASH_HEADS):
        acc = acc_ref[h]
        o_ref[:, h * A_DV:(h + 1) * A_DV] = (acc[:, :A_DV] / acc[:, A_DV:]).astype(BF16)


def _flash(qcat, kn, kpe128, v, batch, seq):
    t = qcat.shape[0]
    tile = min(512, seq)
    nq = seq // tile
    hp = FLASH_HEADS
    return pl.pallas_call(
        functools.partial(_flash_kernel, tile=tile),
        grid=(batch, A_HEADS // hp, nq),
        in_specs=[
            pl.BlockSpec((tile, hp * 2 * LANES), lambda b, h, i: (b * nq + i, h)),
            pl.BlockSpec((seq, hp * A_NOPE), lambda b, h, i: (b, h)),
            pl.BlockSpec((seq, LANES), lambda b, h, i: (b, 0)),
            pl.BlockSpec((seq, hp * A_DV), lambda b, h, i: (b, h)),
        ],
        out_specs=pl.BlockSpec((tile, hp * A_DV), lambda b, h, i: (b * nq + i, h)),
        out_shape=jax.ShapeDtypeStruct((t, A_HEADS * A_DV), BF16),
        scratch_shapes=[pltpu.VMEM((hp, tile, LANES), F32), pltpu.VMEM((hp, tile, 2 * A_DV), F32)],
        compiler_params=_params(("arbitrary", "arbitrary", "arbitrary")),
        name="mla_flash",
    )(qcat, kn, kpe128, v)


def _decode_kernel(pt_ref, qabs_ref, qcat_ref, ckvn_ref, kpen_ref, kv_hbm, kr_hbm, o_ref,
                   kv_buf, kr_buf, kv_sem, kr_sem, *, layer, pages, n_chunks):
    b = pl.program_id(0)
    span = pages * PAGE_SIZE

    def kv_copy(page, slot, i):
        return pltpu.make_async_copy(kv_hbm.at[layer, page], kv_buf.at[slot, pl.ds(i * PAGE_SIZE, PAGE_SIZE), :],
                                     kv_sem.at[slot])

    def kr_copy(page, slot, i):
        return pltpu.make_async_copy(kr_hbm.at[layer, page], kr_buf.at[slot, :, pl.ds(i * PAGE_SIZE, PAGE_SIZE)],
                                     kr_sem.at[slot])

    def start_chunk(seq, c, slot):
        for i in range(pages):
            page = pt_ref[seq, c * pages + i]
            kv_copy(page, slot, i).start()
            kr_copy(page, slot, i).start()

    def wait_chunk(slot):
        for i in range(pages):
            kv_copy(0, slot, i).wait()
            kr_copy(0, slot, i).wait()

    @pl.when(b == 0)
    def _():
        for c in range(DECODE_AHEAD):
            start_chunk(0, c, c % DECODE_SLOTS)

    qa = qabs_ref[...]
    qp = qcat_ref[:, A_NOPE:A_NOPE + A_ROPE]
    m = jnp.full((A_HEADS, 1), -jnp.inf, F32)
    l = jnp.zeros((A_HEADS, 1), F32)
    acc = jnp.zeros((A_HEADS, KV_RANK), F32)
    for c in range(n_chunks):
        slot = c % DECODE_SLOTS
        ahead = c + DECODE_AHEAD
        if ahead < n_chunks:
            start_chunk(b, ahead, ahead % DECODE_SLOTS)
        else:
            @pl.when(b + 1 < pl.num_programs(0))
            def _():
                start_chunk(b + 1, ahead - n_chunks, ahead % DECODE_SLOTS)
        wait_chunk(slot)
        kvb = kv_buf[slot].astype(BF16)
        krb = kr_buf[slot].astype(BF16)
        s = (lax.dot_general(qa, kvb, _NT, preferred_element_type=F32)
             + jnp.dot(qp, krb, preferred_element_type=F32)) * MLA_SCALE
        m_new = jnp.maximum(m, jnp.max(s, axis=1, keepdims=True))
        alpha = jnp.exp(m - m_new)
        p = jnp.exp(s - m_new)
        l = alpha * l + jnp.sum(p, axis=1, keepdims=True)
        acc = alpha * acc + jnp.dot(p.astype(BF16), kvb, preferred_element_type=F32)
        m = m_new

    ckv_n = ckvn_ref[...]
    kpe_n = kpen_ref[...]
    s_n = (jnp.sum(qa.astype(F32) * ckv_n, axis=1, keepdims=True)
           + jnp.sum(qp.astype(F32) * kpe_n, axis=1, keepdims=True)) * MLA_SCALE
    m_f = jnp.maximum(m, s_n)
    a_past = jnp.exp(m - m_f)
    a_new = jnp.exp(s_n - m_f)
    o_ref[...] = (acc * a_past + a_new * ckv_n) / (l * a_past + a_new)


def _decode_attention(page_table, qabs, qcat, ckv_new, kpe_new, cache_kv, cache_kr_t, layer):
    batch, n_pages = page_table.shape
    pages = PAGES_PER_STEP
    n_chunks = n_pages // pages
    assert n_pages % pages == 0 and n_chunks % DECODE_SLOTS == 0 and DECODE_AHEAD < DECODE_SLOTS <= n_chunks
    qabs3 = qabs.reshape(batch, A_HEADS, KV_RANK)
    qcat3 = qcat.reshape(batch, A_HEADS, 2 * LANES)
    span = pages * PAGE_SIZE

    per_b = lambda shape: pl.BlockSpec((None,) + shape, lambda b, pt: (b, 0, 0))
    hbm = pl.BlockSpec(memory_space=pl.ANY)
    grid_spec = pltpu.PrefetchScalarGridSpec(
        num_scalar_prefetch=1,
        grid=(batch,),
        in_specs=[per_b((A_HEADS, KV_RANK)), per_b((A_HEADS, 2 * LANES)), per_b((1, KV_RANK)), per_b((1, A_ROPE)),
                  hbm, hbm],
        out_specs=per_b((A_HEADS, KV_RANK)),
        scratch_shapes=[pltpu.VMEM((DECODE_SLOTS, span, KV_RANK), F32),
                        pltpu.VMEM((DECODE_SLOTS, A_ROPE, span), F32),
                        pltpu.SemaphoreType.DMA((DECODE_SLOTS,)),
                        pltpu.SemaphoreType.DMA((DECODE_SLOTS,))],
    )
    o_lat = pl.pallas_call(
        functools.partial(_decode_kernel, layer=layer, pages=pages, n_chunks=n_chunks),
        grid_spec=grid_spec,
        out_shape=jax.ShapeDtypeStruct((batch, A_HEADS, KV_RANK), F32),
        compiler_params=_params(("arbitrary",)),
        name="mla_decode",
    )(page_table, qabs3, qcat3, ckv_new.reshape(batch, 1, KV_RANK), kpe_new.reshape(batch, 1, A_ROPE),
      cache_kv, cache_kr_t)
    return o_lat.reshape(batch, A_HEADS * KV_RANK)


def _latent_out_kernel(o_ref, wuv_ref, ya_ref):
    for h in range(A_HEADS):
        ya_ref[:, h * A_DV:(h + 1) * A_DV] = jnp.dot(
            o_ref[:, h * KV_RANK:(h + 1) * KV_RANK].astype(BF16), wuv_ref[:, h * A_DV:(h + 1) * A_DV],
            preferred_element_type=F32).astype(BF16)


def _latent_out(o_lat, wuv, layer):
    batch = o_lat.shape[0]
    return pl.pallas_call(
        _latent_out_kernel,
        grid=(1,),
        in_specs=[pl.BlockSpec(o_lat.shape, lambda i: (0, 0)),
                  pl.BlockSpec((None,) + wuv.shape[1:], lambda i: (layer, 0, 0))],
        out_specs=pl.BlockSpec((batch, A_HEADS * A_DV), lambda i: (0, 0)),
        out_shape=jax.ShapeDtypeStruct((batch, A_HEADS * A_DV), BF16),
        compiler_params=_params(("arbitrary",)),
        name="mla_latent_out",
    )(o_lat, wuv)


def _out_proj_kernel(ym_ref, ya_ref, x_ref, w1_ref, w2_ref, gpost_ref, ga_ref, gpre_ref, sc_ref, sh_ref,
                     xo_ref, h_ref):
    y = jnp.dot(ym_ref[...], w1_ref[...], preferred_element_type=F32) \
        + jnp.dot(ya_ref[...], w2_ref[...], preferred_element_type=F32)
    xn = x_ref[...] + ga_ref[...] * (_rms(y) * gpost_ref[...])
    xo_ref[...] = xn
    h_ref[...] = ((_rms(xn) * gpre_ref[...]) * (1.0 + sc_ref[...]) + sh_ref[...]).astype(BF16)


def _out_proj(ym, ya, x, w_out, g_post, ga, g_pre, sc, sh, layer):
    t, d = x.shape
    half = ym.shape[1]
    tm = min(256, t)
    vec = pl.BlockSpec((1, d), lambda i: (0, 0))
    return pl.pallas_call(
        _out_proj_kernel,
        grid=(t // tm,),
        in_specs=[
            pl.BlockSpec((tm, half), lambda i: (i, 0)),
            pl.BlockSpec((tm, half), lambda i: (i, 0)),
            pl.BlockSpec((tm, d), lambda i: (i, 0)),
            pl.BlockSpec((None, half, d), lambda i: (layer, 0, 0)),
            pl.BlockSpec((None, half, d), lambda i: (layer, 1, 0)),
            vec, _mod_spec(ga, tm, t), vec, _mod_spec(sc, tm, t), _mod_spec(sh, tm, t),
        ],
        out_specs=[pl.BlockSpec((tm, d), lambda i: (i, 0)), pl.BlockSpec((tm, d), lambda i: (i, 0))],
        out_shape=[jax.ShapeDtypeStruct((t, d), F32), jax.ShapeDtypeStruct((t, d), BF16)],
        compiler_params=_params(("arbitrary",)),
        name="out_proj",
    )(ym, ya, x, w_out, w_out, g_post, ga, g_pre, sc, sh)


def _mlp_kernel(h_ref, wu_ref, wd_ref, x_ref, gpost_ref, ga_ref, xo_ref, acc_ref):
    f = pl.program_id(1)

    @pl.when(f == 0)
    def _():
        acc_ref[...] = jnp.zeros_like(acc_ref)

    a = jnp.dot(h_ref[...], wu_ref[...], preferred_element_type=F32)
    a = jnp.square(jnp.maximum(a, 0.0)).astype(BF16)
    acc_ref[...] += jnp.dot(a, wd_ref[...], preferred_element_type=F32)

    @pl.when(f == pl.num_programs(1) - 1)
    def _():
        xo_ref[...] = x_ref[...] + ga_ref[...] * (_rms(acc_ref[...]) * gpost_ref[...])


def _mlp(h, x, w_up, w_down, g_post, ga, layer):
    t, d = x.shape
    ff = w_up.shape[2]
    tm = min(512, t)
    tf = 1024
    return pl.pallas_call(
        _mlp_kernel,
        grid=(t // tm, ff // tf),
        in_specs=[
            pl.BlockSpec((tm, d), lambda i, f: (i, 0)),
            pl.BlockSpec((None, d, tf), lambda i, f: (layer, 0, f)),
            pl.BlockSpec((None, tf, d), lambda i, f: (layer, f, 0)),
            pl.BlockSpec((tm, d), lambda i, f: (i, 0)),
            pl.BlockSpec((1, d), lambda i, f: (0, 0)),
            _mod_spec(ga, tm, t),
        ],
        out_specs=pl.BlockSpec((tm, d), lambda i, f: (i, 0)),
        out_shape=jax.ShapeDtypeStruct((t, d), F32),
        scratch_shapes=[pltpu.VMEM((tm, d), F32)],
        compiler_params=_params(("arbitrary", "arbitrary")),
        name="mlp",
    )(h, w_up, w_down, x, g_post, ga)


def _prep_w_in(w_in):
    qw, vw = M_HEADS * M_DK, M_HEADS * M_DV
    o = 2 * qw + 2 * vw
    gates = w_in[..., o:o + 2 * M_HEADS]
    o += 2 * M_HEADS
    q_lat = w_in[..., o:o + Q_RANK]
    o += Q_RANK
    kv_lat = w_in[..., o:o + KV_RANK]
    o += KV_RANK
    k_r = w_in[..., o:o + A_ROPE]
    half = A_ROPE // 2
    zeros = lambda n: jnp.zeros(w_in.shape[:-1] + (n,), w_in.dtype)
    out = jnp.concatenate([
        w_in[..., :2 * qw + 2 * vw],
        q_lat,
        gates, zeros(LANES - 2 * M_HEADS),
        kv_lat,
        k_r, k_r[..., half:], k_r[..., :half],
        zeros(U_WIDTH - U_KR - LANES),
    ], axis=-1)
    assert out.shape[-1] == U_WIDTH
    return out.astype(BF16)


def _prep_w_uq(w_uq):
    depth, r, _ = w_uq.shape
    w = w_uq.reshape(depth, r, A_HEADS, A_NOPE + A_ROPE)
    half = A_ROPE // 2
    x1 = w[..., A_NOPE:A_NOPE + half]
    x2 = w[..., A_NOPE + half:]
    out = jnp.concatenate([w[..., :A_NOPE], x1, x2, x2, x1], axis=-1)
    return out.reshape(depth, r, A_HEADS * 2 * LANES).astype(BF16)


def _rope_tables(pos):
    freqs = ROPE_THETA ** (-jnp.arange(0, A_ROPE, 2, dtype=F32) / A_ROPE)
    ang = pos[:, None] * freqs[None, :]
    cos, sin = jnp.cos(ang), jnp.sin(ang)
    z = jnp.zeros((pos.shape[0], A_ROPE), F32)
    return jnp.concatenate([cos, cos, z], axis=1), jnp.concatenate([-sin, sin, z], axis=1)


def _split_mod(ada, groups, rows):
    d = ada.shape[1] // 6
    return [ada[:, i * d:(i + 1) * d].reshape(groups, rows, d) for i in range(6)]


def kernel(x_prompt, x_sample, c_prompt, c_sample, cache_kv_latent, cache_k_rope, state_C, state_n, state_m,
           page_table, w_ada, b_ada, g_pre_mix, g_post_mix, g_pre_ff, g_post_ff, w_in, b_i, b_f, g_mh,
           g_q, w_uq, g_kv, w_uk, w_uv, w_out, w_up, w_down):
    bp, lp, d = x_prompt.shape
    bs, ls, _ = x_sample.shape
    assert ls == 1
    depth = w_in.shape[0]
    tp = bp * lp

    w_in_b = _prep_w_in(w_in)
    w_uq_b = _prep_w_uq(w_uq)
    w_uk_b = w_uk.astype(BF16)
    w_uv_b = w_uv.astype(BF16)
    w_out_b = w_out.astype(BF16)
    w_up_b = w_up.astype(BF16)
    w_down_b = w_down.astype(BF16)

    pad = (-(bp + bs)) % 8
    c_all = jnp.concatenate([c_prompt, c_sample, jnp.zeros((pad, d), F32)], axis=0)
    ada = _ada(c_all, w_ada, b_ada)

    cos_p, sin_p = _rope_tables(jnp.arange(lp, dtype=F32))
    cos_s, sin_s = _rope_tables(jnp.full((bs,), PAST_LEN, F32) + jnp.arange(ls, dtype=F32))

    cache_kr_t = jnp.swapaxes(cache_k_rope, 2, 3)

    xp = x_prompt.reshape(tp, d)
    xs = x_sample.reshape(bs, d)
    outs_p = [[] for _ in range(5)]
    outs_s = [[] for _ in range(5)]
    vec = lambda a: a.reshape(1, -1)

    for l in range(depth):
        gains = dict(g_pre=vec(g_pre_mix[l]), g_post=vec(g_post_mix[l]), g_pre_ff=vec(g_pre_ff[l]),
                     g_post_ff=vec(g_post_ff[l]), g_mh=vec(g_mh[l]), g_q=vec(g_q[l]), g_kv=vec(g_kv[l]))

        sh1, sc1, ga1, sh2, sc2, ga2 = _split_mod(ada[l, :bp], bp, 1)
        u = _in_proj(xp, gains["g_pre"], sc1, sh1, w_in_b, l)
        ym, c_new, n_new, m_new = _mlstm_prompt(u, b_i[l], b_f[l], gains["g_mh"], bp, lp)
        qcat, ckv, kpe, kn, v, kpe128 = _mla_proj(u, gains["g_q"], gains["g_kv"], w_uq_b, w_uk_b, w_uv_b,
                                                  cos_p, sin_p, l, prompt=True)
        ya = _flash(qcat, kn, kpe128, v, bp, lp)
        xp, h2 = _out_proj(ym, ya, xp, w_out_b, gains["g_post"], ga1, gains["g_pre_ff"], sc2, sh2, l)
        xp = _mlp(h2, xp, w_up_b, w_down_b, gains["g_post_ff"], ga2, l)
        for acc, val in zip(outs_p, (ckv.reshape(bp, lp, KV_RANK), kpe.reshape(bp, lp, A_ROPE),
                                     c_new, n_new, m_new[:, :, 0])):
            acc.append(val)

        sh1, sc1, ga1, sh2, sc2, ga2 = _split_mod(ada[l, bp:bp + bs], 1, bs)
        u = _in_proj(xs, gains["g_pre"], sc1, sh1, w_in_b, l)
        m0 = jnp.broadcast_to(state_m[l][:, :, None], (bs, M_HEADS, LANES))
        ym, c_new, n_new, m_new = _mlstm_sample(u, b_i[l], b_f[l], gains["g_mh"], state_C, state_n, m0, l)
        qcat, ckv, kpe, qabs = _mla_proj(u, gains["g_q"], gains["g_kv"], w_uq_b, w_uk_b, None,
                                         cos_s, sin_s, l, prompt=False)
        o_lat = _decode_attention(page_table, qabs, qcat, ckv, kpe, cache_kv_latent, cache_kr_t, l)
        ya = _latent_out(o_lat, w_uv_b, l)
        xs, h2 = _out_proj(ym, ya, xs, w_out_b, gains["g_post"], ga1, gains["g_pre_ff"], sc2, sh2, l)
        xs = _mlp(h2, xs, w_up_b, w_down_b, gains["g_post_ff"], ga2, l)
        for acc, val in zip(outs_s, (ckv.reshape(bs, ls, KV_RANK), kpe.reshape(bs, ls, A_ROPE),
                                     c_new, n_new, m_new[:, :, 0])):
            acc.append(val)

    return (xp.reshape(bp, lp, d), xs.reshape(bs, ls, d),
            *[jnp.stack(a) for a in outs_p], *[jnp.stack(a) for a in outs_s])
```

```python
import functools

import jax
import jax.numpy as jnp
import numpy as np
from jax import lax
from jax.experimental import pallas as pl
from jax.experimental.pallas import tpu as pltpu

F32 = jnp.float32
BF16 = jnp.bfloat16

EPS = 1e-6
M_HEADS = 4
M_DK = 128
M_DV = 256
M_CHUNK = 128
A_HEADS = 8
A_DV = 128
A_NOPE = 128
A_ROPE = 64
Q_RANK = 384
KV_RANK = 256
ROPE_THETA = 10000.0
PAST_LEN = 16384
PAGE_SIZE = 128
LANES = 128

U_WIDTH = 4096
U_Q, U_K, U_V, U_O = 0, 512, 1024, 2048
U_QLAT = 3072
U_GATES = 3456
U_KVLAT = 3584
U_KR = 3840

V7X_VMEM_LIMIT = 56 * 1024 * 1024
MLA_SCALE = float((A_NOPE + A_ROPE) ** -0.5)
MLSTM_SCALE = float(M_DK ** -0.5)
PAGES_PER_STEP = 16
DECODE_SLOTS = 4
DECODE_AHEAD = 3
STEP_BATCH = 8
CHUNK_SEQS = 4

_NT = (((1,), (1,)), ((), ()))


def _params(semantics, vmem=V7X_VMEM_LIMIT):
    return pltpu.CompilerParams(dimension_semantics=semantics, vmem_limit_bytes=vmem)


def _rms(x):
    return x * lax.rsqrt(jnp.mean(x * x, axis=-1, keepdims=True) + EPS)


def _log_sigmoid(x):
    return jnp.minimum(x, 0.0) - jnp.log1p(jnp.exp(-jnp.abs(x)))


def _swap_halves(x):
    return jnp.concatenate([x[:, 64:], x[:, :64]], axis=1)


def _ada_kernel(c_ref, w_ref, b_ref, o_ref):
    c = c_ref[...]
    a = (c * jax.nn.sigmoid(c)).astype(BF16)
    o_ref[...] = jnp.dot(a, w_ref[...].astype(BF16), preferred_element_type=F32) + b_ref[...]


def _ada(c_all, w_ada, b_ada):
    depth, d, n = w_ada.shape
    r = c_all.shape[0]
    tn = 1024
    return pl.pallas_call(
        _ada_kernel,
        grid=(depth, n // tn),
        in_specs=[
            pl.BlockSpec((r, d), lambda l, j: (0, 0)),
            pl.BlockSpec((None, d, tn), lambda l, j: (l, 0, j)),
            pl.BlockSpec((None, 1, tn), lambda l, j: (l, 0, j)),
        ],
        out_specs=pl.BlockSpec((None, r, tn), lambda l, j: (l, 0, j)),
        out_shape=jax.ShapeDtypeStruct((depth, r, n), F32),
        compiler_params=_params(("arbitrary", "arbitrary")),
        name="ada",
    )(c_all, w_ada, b_ada.reshape(depth, 1, n))


def _in_proj_kernel(x_ref, g_ref, sc_ref, sh_ref, w_ref, o_ref, h_ref):
    @pl.when(pl.program_id(1) == 0)
    def _():
        h = (_rms(x_ref[...]) * g_ref[...]) * (1.0 + sc_ref[...]) + sh_ref[...]
        h_ref[...] = h.astype(BF16)

    o_ref[...] = jnp.dot(h_ref[...], w_ref[...], preferred_element_type=F32)


def _mod_spec(mod, tm, t):
    g, r, d = mod.shape
    rows_per_group = t // g
    assert r in (1, tm) and rows_per_group % tm == 0
    return pl.BlockSpec((None, r, d), lambda i, *_: ((i * tm) // rows_per_group, 0, 0))


def _in_proj_ahead_kernel(x0_ref, xq_ref, g_ref, sc0_ref, sh0_ref, scn_ref, shn_ref, w_ref, o_ref, ha_ref, hb_ref,
                          *, part):
    i = pl.program_id(0)
    j = pl.program_id(1)

    def modulate(x, sc, sh):
        return ((_rms(x) * g_ref[...]) * (1.0 + sc) + sh).astype(BF16)

    @pl.when(jnp.logical_and(i == 0, j == 0))
    def _():
        ha_ref[...] = modulate(x0_ref[...], sc0_ref[...], sh0_ref[...])

    rows = pl.ds(pl.multiple_of(j * part, part), part)

    def step(cur_ref, nxt_ref):
        nxt_ref[rows, :] = modulate(xq_ref[...], scn_ref[...], shn_ref[...])
        o_ref[...] = jnp.dot(cur_ref[...], w_ref[...], preferred_element_type=F32)

    @pl.when(i % 2 == 0)
    def _():
        step(ha_ref, hb_ref)

    @pl.when(i % 2 == 1)
    def _():
        step(hb_ref, ha_ref)


def _in_proj_ahead(x, g, sc, sh, w, layer, tm, tn):
    t, d = x.shape
    n = w.shape[2]
    nt, nj = t // tm, n // tn
    part = tm // nj
    groups = sc.shape[0]
    rows_per_group = t // groups
    assert sc.shape[1] == 1 and rows_per_group % tm == 0 and tm % nj == 0
    ahead = lambda i: jnp.minimum(i + 1, nt - 1)
    mod0 = pl.BlockSpec((None, 1, d), lambda i, j: (0, 0, 0))
    modn = pl.BlockSpec((None, 1, d), lambda i, j: ((ahead(i) * tm) // rows_per_group, 0, 0))
    return pl.pallas_call(
        functools.partial(_in_proj_ahead_kernel, part=part),
        grid=(nt, nj),
        in_specs=[
            pl.BlockSpec((tm, d), lambda i, j: (0, 0)),
            pl.BlockSpec((part, d), lambda i, j: (ahead(i) * nj + j, 0)),
            pl.BlockSpec((1, d), lambda i, j: (0, 0)),
            mod0, mod0, modn, modn,
            pl.BlockSpec((None, d, tn), lambda i, j: (layer, 0, j)),
        ],
        out_specs=pl.BlockSpec((tm, tn), lambda i, j: (i, j)),
        out_shape=jax.ShapeDtypeStruct((t, n), F32),
        scratch_shapes=[pltpu.VMEM((tm, d), BF16), pltpu.VMEM((tm, d), BF16)],
        compiler_params=_params(("arbitrary", "arbitrary")),
        name="in_proj",
    )(x, x, g, sc, sh, sc, sh, w)


def _in_proj(x, g, sc, sh, w, layer):
    t, d = x.shape
    n = w.shape[2]
    tm = min(1024, t)
    tn = 1024
    if t // tm > 1 and sc.shape[1] == 1:
        return _in_proj_ahead(x, g, sc, sh, w, layer, tm, tn)
    return pl.pallas_call(
        _in_proj_kernel,
        grid=(t // tm, n // tn),
        in_specs=[
            pl.BlockSpec((tm, d), lambda i, j: (i, 0)),
            pl.BlockSpec((1, d), lambda i, j: (0, 0)),
            _mod_spec(sc, tm, t),
            _mod_spec(sh, tm, t),
            pl.BlockSpec((None, d, tn), lambda i, j: (layer, 0, j)),
        ],
        out_specs=pl.BlockSpec((tm, tn), lambda i, j: (i, j)),
        out_shape=jax.ShapeDtypeStruct((t, n), F32),
        scratch_shapes=[pltpu.VMEM((tm, d), BF16)],
        compiler_params=_params(("arbitrary", "arbitrary")),
        name="in_proj",
    )(x, g, sc, sh, w)


def _head_out(hh, gmh, o_pre):
    return ((_rms(hh) * gmh) * jax.nn.sigmoid(o_pre)).astype(BF16)


def _mlstm_chunk_kernel(bi_ref, bf_ref, q_ref, k_ref, v_ref, o_ref, g_ref, gmh_ref,
                        ym_ref, cout_ref, nout_ref, mout_ref, ct_ref, n_ref, m_ref, *, seqs):
    c = pl.program_id(1)
    cs = M_CHUNK

    @pl.when(c == 0)
    def _():
        ct_ref[...] = jnp.zeros_like(ct_ref)
        n_ref[...] = jnp.zeros_like(n_ref)
        m_ref[...] = jnp.zeros_like(m_ref)

    lane = lax.broadcasted_iota(jnp.int32, (1, LANES), 1)
    bias = jnp.zeros((1, LANES), F32)
    for h in range(M_HEADS):
        bias = jnp.where(lane == h, bi_ref[h], bias)
        bias = jnp.where(lane == M_HEADS + h, bf_ref[h], bias)
    row = lax.broadcasted_iota(jnp.int32, (cs, cs), 0)
    col = lax.broadcasted_iota(jnp.int32, (cs, cs), 1)
    causal = col <= row

    def gates(sq):
        g = g_ref[sq] + bias
        gt = g.T[:2 * M_HEADS, :]
        return g, gt, _log_sigmoid(g), _log_sigmoid(gt)

    def chain(sq, h, g, gt, ls, lst):
        st = sq * M_HEADS + h
        i_col = g[:, h:h + 1]
        f_col = ls[:, M_HEADS + h:M_HEADS + h + 1]
        i_row = gt[h:h + 1, :]
        f_row = lst[M_HEADS + h:M_HEADS + h + 1, :]
        q = q_ref[sq, :, h * M_DK:(h + 1) * M_DK]
        k = k_ref[sq, :, h * M_DK:(h + 1) * M_DK]
        v = v_ref[sq, :, h * M_DV:(h + 1) * M_DV]
        ct = ct_ref[st]
        n = n_ref[st]
        m_prev = m_ref[st][:, :1]
        qb = q.astype(BF16)
        ktb = k.T.astype(BF16)
        b_col = jnp.sum(jnp.where(causal, f_row, 0.0), axis=1, keepdims=True)
        b_row = jnp.sum(jnp.where(row <= col, f_col, 0.0), axis=0, keepdims=True)
        qk = jnp.dot(qb, ktb, preferred_element_type=F32)
        inter = jnp.dot(qb, ct.astype(BF16), preferred_element_type=F32)
        qn = jnp.sum(q * n, axis=1, keepdims=True)
        yield
        dm = jnp.where(causal, b_col - b_row + i_row, -jnp.inf)
        log_inter = b_col + m_prev
        dm_max = jnp.max(dm, axis=1, keepdims=True)
        yield
        m_t = jnp.maximum(log_inter, dm_max)
        w_inter = jnp.exp(log_inter - m_t) * MLSTM_SCALE
        s = qk * (jnp.exp(dm - m_t) * MLSTM_SCALE)
        s_sum = jnp.sum(s, axis=1, keepdims=True)
        num = w_inter * inter + jnp.dot(s.astype(BF16), v.astype(BF16), preferred_element_type=F32)
        m_new = m_t[cs - 1:cs, :]
        b_last = b_col[cs - 1:cs, :]
        a_inter = jnp.exp(b_last + m_prev - m_new)
        a_col = jnp.exp(b_last - b_col + i_col - m_new)
        ct_new = a_inter * ct + jnp.dot(ktb, (v * a_col).astype(BF16), preferred_element_type=F32)
        n_new = a_inter * n + jnp.sum(a_col * k, axis=0, keepdims=True)
        yield
        hh = num / jnp.maximum(jnp.abs(w_inter * qn + s_sum), jnp.exp(-m_t))
        ms = jnp.mean(hh * hh, axis=-1, keepdims=True)
        yield
        sl = slice(h * M_DV, (h + 1) * M_DV)
        out = ((hh * lax.rsqrt(ms + EPS)) * gmh_ref[:, sl]) * jax.nn.sigmoid(o_ref[sq, :, sl])
        ct_ref[st] = ct_new
        n_ref[st] = n_new
        m_ref[st] = jnp.broadcast_to(m_new, (1, LANES))
        ym_ref[sq, :, sl] = out.astype(BF16)

    chains = []
    for sq in range(seqs):
        gate_vals = gates(sq)
        chains += [chain(sq, h, *gate_vals) for h in range(M_HEADS)]
    for _ in range(4):
        for ch in chains:
            next(ch)
    for ch in chains:
        assert next(ch, None) is None

    @pl.when(c == pl.num_programs(1) - 1)
    def _():
        for sq in range(seqs):
            for h in range(M_HEADS):
                st = sq * M_HEADS + h
                cout_ref[sq, h] = ct_ref[st].T
                nout_ref[sq, h:h + 1, :] = n_ref[st]
                mout_ref[sq, h:h + 1, :] = m_ref[st]


def _mlstm_prompt(u, b_i, b_f, g_mh, batch, seq):
    t = u.shape[0]
    nc = seq // M_CHUNK
    cs = M_CHUNK
    sb = min(CHUNK_SEQS, batch)
    assert batch % sb == 0
    qw, vw = M_HEADS * M_DK, M_HEADS * M_DV
    u3 = u.reshape(batch, seq, U_WIDTH)
    smem = pl.BlockSpec(memory_space=pltpu.SMEM)
    ym, c_new, n_new, m_new = pl.pallas_call(
        functools.partial(_mlstm_chunk_kernel, seqs=sb),
        grid=(batch // sb, nc),
        in_specs=[
            smem, smem,
            pl.BlockSpec((sb, cs, qw), lambda b, c: (b, c, U_Q // qw)),
            pl.BlockSpec((sb, cs, qw), lambda b, c: (b, c, U_K // qw)),
            pl.BlockSpec((sb, cs, vw), lambda b, c: (b, c, U_V // vw)),
            pl.BlockSpec((sb, cs, vw), lambda b, c: (b, c, U_O // vw)),
            pl.BlockSpec((sb, cs, LANES), lambda b, c: (b, c, U_GATES // LANES)),
            pl.BlockSpec((1, vw), lambda b, c: (0, 0)),
        ],
        out_specs=[
            pl.BlockSpec((sb, cs, vw), lambda b, c: (b, c, 0)),
            pl.BlockSpec((sb, M_HEADS, M_DV, M_DK), lambda b, c: (b, 0, 0, 0)),
            pl.BlockSpec((sb, M_HEADS, M_DK), lambda b, c: (b, 0, 0)),
            pl.BlockSpec((sb, M_HEADS, LANES), lambda b, c: (b, 0, 0)),
        ],
        out_shape=[
            jax.ShapeDtypeStruct((batch, seq, vw), BF16),
            jax.ShapeDtypeStruct((batch, M_HEADS, M_DV, M_DK), F32),
            jax.ShapeDtypeStruct((batch, M_HEADS, M_DK), F32),
            jax.ShapeDtypeStruct((batch, M_HEADS, LANES), F32),
        ],
        scratch_shapes=[
            pltpu.VMEM((sb * M_HEADS, M_DK, M_DV), F32),
            pltpu.VMEM((sb * M_HEADS, 1, M_DK), F32),
            pltpu.VMEM((sb * M_HEADS, 1, LANES), F32),
        ],
        compiler_params=_params(("arbitrary", "arbitrary")),
        name="mlstm_prompt",
    )(b_i, b_f, u3, u3, u3, u3, u3, g_mh)
    return ym.reshape(t, vw), c_new, n_new, m_new


def _mlstm_step_kernel(bi_ref, bf_ref, q_ref, k_ref, v_ref, o_ref, g_ref, gmh_ref, c0_ref, n0_ref, m0_ref,
                       ym_ref, cout_ref, nout_ref, mout_ref):
    eye = jnp.where(lax.broadcasted_iota(jnp.int32, (M_DV, M_DV), 0)
                    == lax.broadcasted_iota(jnp.int32, (M_DV, M_DV), 1), 1.0, 0.0).astype(BF16)
    for b, h in [(b, h) for b in range(STEP_BATCH) for h in range(M_HEADS)]:
        g = g_ref[b]
        ig = g[:, h:h + 1] + bi_ref[h]
        lf = _log_sigmoid(g[:, M_HEADS + h:M_HEADS + h + 1] + bf_ref[h])
        m0 = m0_ref[b, h:h + 1, :1]
        log_inter = lf + m0
        m_t = jnp.maximum(log_inter, ig)
        w_inter = jnp.exp(log_inter - m_t)
        p = jnp.exp(ig - m_t)

        q = q_ref[b, :, h * M_DK:(h + 1) * M_DK]
        k = k_ref[b, :, h * M_DK:(h + 1) * M_DK]
        v = v_ref[b, :, h * M_DV:(h + 1) * M_DV]
        c0 = c0_ref[b, h]
        n0 = n0_ref[b, h:h + 1, :]

        s = jnp.sum(q * k, axis=1, keepdims=True) * (p * MLSTM_SCALE)
        q8 = jnp.broadcast_to(q, (8, M_DK)).astype(BF16)
        cq = lax.dot_general(q8, c0.astype(BF16), _NT, preferred_element_type=F32)[:1, :]
        wq = w_inter * MLSTM_SCALE
        num = wq * cq + s * v
        den = wq * jnp.sum(n0 * q, axis=1, keepdims=True) + s
        hh = num / jnp.maximum(jnp.abs(den), jnp.exp(-m_t))

        pv = jnp.broadcast_to(p * v, (LANES, M_DV)).astype(BF16)
        pv_col = lax.dot_general(eye, pv, _NT, preferred_element_type=F32)
        cout_ref[b, h] = w_inter * c0 + pv_col * k
        nout_ref[b, h:h + 1, :] = w_inter * n0 + p * k
        mout_ref[b, h:h + 1, :] = jnp.broadcast_to(m_t, (1, LANES))

        sl = slice(h * M_DV, (h + 1) * M_DV)
        ym_ref[b, :, sl] = _head_out(hh, gmh_ref[:, sl], o_ref[b, :, sl])


def _mlstm_sample(u, b_i, b_f, g_mh, c0, n0, m0, layer):
    batch = u.shape[0]
    sb = STEP_BATCH
    assert batch % sb == 0
    qw, vw = M_HEADS * M_DK, M_HEADS * M_DV
    u3 = u.reshape(batch, 1, U_WIDTH)
    smem = pl.BlockSpec(memory_space=pltpu.SMEM)
    ym, c, n, m = pl.pallas_call(
        _mlstm_step_kernel,
        grid=(batch // sb,),
        in_specs=[
            smem, smem,
            pl.BlockSpec((sb, 1, qw), lambda b: (b, 0, U_Q // qw)),
            pl.BlockSpec((sb, 1, qw), lambda b: (b, 0, U_K // qw)),
            pl.BlockSpec((sb, 1, vw), lambda b: (b, 0, U_V // vw)),
            pl.BlockSpec((sb, 1, vw), lambda b: (b, 0, U_O // vw)),
            pl.BlockSpec((sb, 1, LANES), lambda b: (b, 0, U_GATES // LANES)),
            pl.BlockSpec((1, vw), lambda b: (0, 0)),
            pl.BlockSpec((None, sb, M_HEADS, M_DV, M_DK), lambda b: (layer, b, 0, 0, 0)),
            pl.BlockSpec((None, sb, M_HEADS, M_DK), lambda b: (layer, b, 0, 0)),
            pl.BlockSpec((sb, M_HEADS, LANES), lambda b: (b, 0, 0)),
        ],
        out_specs=[
            pl.BlockSpec((sb, 1, vw), lambda b: (b, 0, 0)),
            pl.BlockSpec((sb, M_HEADS, M_DV, M_DK), lambda b: (b, 0, 0, 0)),
            pl.BlockSpec((sb, M_HEADS, M_DK), lambda b: (b, 0, 0)),
            pl.BlockSpec((sb, M_HEADS, LANES), lambda b: (b, 0, 0)),
        ],
        out_shape=[
            jax.ShapeDtypeStruct((batch, 1, vw), BF16),
            jax.ShapeDtypeStruct((batch, M_HEADS, M_DV, M_DK), F32),
            jax.ShapeDtypeStruct((batch, M_HEADS, M_DK), F32),
            jax.ShapeDtypeStruct((batch, M_HEADS, LANES), F32),
        ],
        compiler_params=_params(("arbitrary",)),
        name="mlstm_sample",
    )(b_i, b_f, u3, u3, u3, u3, u3, g_mh, c0, n0, m0)
    return ym.reshape(batch, vw), c, n, m


def _mla_common(ql_ref, kvl_ref, kr_ref, gq_ref, gkv_ref, wq_ref, cos_ref, sin_ref,
                qcat_ref, ckv_ref, kpe_ref):
    cos = cos_ref[...]
    sin = sin_ref[...]
    hq = (_rms(ql_ref[...]) * gq_ref[...]).astype(BF16)
    q = jnp.dot(hq, wq_ref[...], preferred_element_type=F32)
    for h in range(A_HEADS):
        base = h * 2 * LANES
        qr = q[:, base + LANES:base + 2 * LANES]
        qcat_ref[:, base:base + LANES] = q[:, base:base + LANES].astype(BF16)
        qcat_ref[:, base + LANES:base + 2 * LANES] = (qr * cos + _swap_halves(qr) * sin).astype(BF16)
    ckv = _rms(kvl_ref[...]) * gkv_ref[...]
    ckv_ref[...] = ckv
    kr = kr_ref[...]
    kpe128 = kr * cos + _swap_halves(kr) * sin
    kpe_ref[...] = kpe128[:, :A_ROPE]
    return q, ckv, kpe128


def _mla_proj_prompt_kernel(ql_ref, kvl_ref, kr_ref, gq_ref, gkv_ref, wq_ref, wuk_ref, wuv_ref, cos_ref, sin_ref,
                            qcat_ref, ckv_ref, kpe_ref, kn_ref, v_ref, kpe128_ref):
    _, ckv, kpe128 = _mla_common(ql_ref, kvl_ref, kr_ref, gq_ref, gkv_ref, wq_ref, cos_ref, sin_ref,
                                 qcat_ref, ckv_ref, kpe_ref)
    cb = ckv.astype(BF16)
    kn_ref[...] = jnp.dot(cb, wuk_ref[...], preferred_element_type=F32).astype(BF16)
    v_ref[...] = jnp.dot(cb, wuv_ref[...], preferred_element_type=F32).astype(BF16)
    kpe128_ref[...] = kpe128.astype(BF16)


def _mla_proj_sample_kernel(ql_ref, kvl_ref, kr_ref, gq_ref, gkv_ref, wq_ref, wuk_ref, cos_ref, sin_ref,
                            qcat_ref, ckv_ref, kpe_ref, qabs_ref):
    q, _, _ = _mla_common(ql_ref, kvl_ref, kr_ref, gq_ref, gkv_ref, wq_ref, cos_ref, sin_ref,
                          qcat_ref, ckv_ref, kpe_ref)
    for h in range(A_HEADS):
        qn = q[:, h * 2 * LANES:h * 2 * LANES + A_NOPE].astype(BF16)
        wk = wuk_ref[:, h * A_NOPE:(h + 1) * A_NOPE]
        qabs_ref[:, h * KV_RANK:(h + 1) * KV_RANK] = lax.dot_general(
            qn, wk, _NT, preferred_element_type=F32).astype(BF16)


def _mla_proj(u, g_q, g_kv, wq, wuk, wuv, cos, sin, layer, *, prompt):
    t = u.shape[0]
    tm = min(512, t)
    npos = cos.shape[0] // tm
    hw = A_HEADS * 2 * LANES
    full = lambda a: pl.BlockSpec(a.shape, lambda i: (0, 0))
    of_layer = lambda a: pl.BlockSpec((None,) + a.shape[1:], lambda i: (layer, 0, 0))
    in_specs = [
        pl.BlockSpec((tm, Q_RANK), lambda i: (i, U_QLAT // Q_RANK)),
        pl.BlockSpec((tm, KV_RANK), lambda i: (i, U_KVLAT // KV_RANK)),
        pl.BlockSpec((tm, LANES), lambda i: (i, U_KR // LANES)),
        full(g_q), full(g_kv), of_layer(wq), of_layer(wuk),
    ]
    args = [u, u, u, g_q, g_kv, wq, wuk]
    if prompt:
        in_specs.append(of_layer(wuv))
        args.append(wuv)
    in_specs += [pl.BlockSpec((tm, LANES), lambda i: (i % npos, 0))] * 2
    args += [cos, sin]
    row = lambda w: pl.BlockSpec((tm, w), lambda i: (i, 0))
    out_specs = [row(hw), row(KV_RANK), row(A_ROPE)]
    out_shape = [jax.ShapeDtypeStruct((t, hw), BF16), jax.ShapeDtypeStruct((t, KV_RANK), F32),
                 jax.ShapeDtypeStruct((t, A_ROPE), F32)]
    if prompt:
        out_specs += [row(A_HEADS * A_NOPE), row(A_HEADS * A_DV), row(LANES)]
        out_shape += [jax.ShapeDtypeStruct((t, A_HEADS * A_NOPE), BF16),
                      jax.ShapeDtypeStruct((t, A_HEADS * A_DV), BF16),
                      jax.ShapeDtypeStruct((t, LANES), BF16)]
        body = _mla_proj_prompt_kernel
    else:
        out_specs += [row(A_HEADS * KV_RANK)]
        out_shape += [jax.ShapeDtypeStruct((t, A_HEADS * KV_RANK), BF16)]
        body = _mla_proj_sample_kernel
    return pl.pallas_call(
        body,
        grid=(t // tm,),
        in_specs=in_specs,
        out_specs=out_specs,
        out_shape=out_shape,
        compiler_params=_params(("arbitrary",)),
        name="mla_proj_prompt" if prompt else "mla_proj_sample",
    )(*args)


FLASH_HEADS = 8
EXP2_SCALE = float(MLA_SCALE * np.log2(np.e))


def _flash_kernel(q_ref, kn_ref, kpe_ref, v_ref, o_ref, m_ref, acc_ref, *, tile):
    qi = pl.program_id(2)
    m_ref[...] = jnp.full_like(m_ref, -jnp.inf)
    acc_ref[...] = jnp.zeros_like(acc_ref)
    ones = jnp.ones((tile, LANES), BF16)
    reps = tile // LANES

    def step(ki, diagonal):
        off = pl.multiple_of(ki * tile, tile)
        kpe = kpe_ref[pl.ds(off, tile), :]
        if diagonal:
            row = lax.broadcasted_iota(jnp.int32, (tile, tile), 0)
            col = lax.broadcasted_iota(jnp.int32, (tile, tile), 1)
            keep = col <= row
        for h in range(FLASH_HEADS):
            q = q_ref[:, h * 2 * LANES:(h + 1) * 2 * LANES]
            k = jnp.concatenate([kn_ref[pl.ds(off, tile), h * A_NOPE:(h + 1) * A_NOPE], kpe], axis=1)
            v = jnp.concatenate([v_ref[pl.ds(off, tile), h * A_DV:(h + 1) * A_DV], ones], axis=1)
            s = lax.dot_general(q, k, _NT, preferred_element_type=F32)
            if diagonal:
                s = jnp.where(keep, s, -jnp.inf)
            m_prev = m_ref[h]
            m_new = jnp.maximum(m_prev, jnp.max(s, axis=1, keepdims=True))
            alpha = jnp.exp2((m_prev - m_new) * EXP2_SCALE)
            p = jnp.exp2((s - jnp.concatenate([m_new] * reps, axis=1)) * EXP2_SCALE)
            acc_ref[h] = jnp.concatenate([alpha, alpha], axis=1) * acc_ref[h] \
                + jnp.dot(p.astype(BF16), v, preferred_element_type=F32)
            m_ref[h] = m_new

    def body(ki, carry):
        step(ki, False)
        return carry

    lax.fori_loop(0, qi, body, 0)
    step(qi, True)
    for h in range(FLASH_HEADS):
        acc = acc_ref[h]
        o_ref[:, h * A_DV:(h + 1) * A_DV] = (acc[:, :A_DV] / acc[:, A_DV:]).astype(BF16)


def _flash(qcat, kn, kpe128, v, batch, seq):
    t = qcat.shape[0]
    tile = min(512, seq)
    nq = seq // tile
    hp = FLASH_HEADS
    return pl.pallas_call(
        functools.partial(_flash_kernel, tile=tile),
        grid=(batch, A_HEADS // hp, nq),
        in_specs=[
            pl.BlockSpec((tile, hp * 2 * LANES), lambda b, h, i: (b * nq + i, h)),
            pl.BlockSpec((seq, hp * A_NOPE), lambda b, h, i: (b, h)),
            pl.BlockSpec((seq, LANES), lambda b, h, i: (b, 0)),
            pl.BlockSpec((seq, hp * A_DV), lambda b, h, i: (b, h)),
        ],
        out_specs=pl.BlockSpec((tile, hp * A_DV), lambda b, h, i: (b * nq + i, h)),
        out_shape=jax.ShapeDtypeStruct((t, A_HEADS * A_DV), BF16),
        scratch_shapes=[pltpu.VMEM((hp, tile, LANES), F32), pltpu.VMEM((hp, tile, 2 * A_DV), F32)],
        compiler_params=_params(("arbitrary", "arbitrary", "arbitrary")),
        name="mla_flash",
    )(qcat, kn, kpe128, v)


def _decode_kernel(pt_ref, qabs_ref, qcat_ref, ckvn_ref, kpen_ref, kv_hbm, kr_hbm, o_ref,
                   kv_buf, kr_buf, kv_sem, kr_sem, *, layer, pages, n_chunks):
    b = pl.program_id(0)
    span = pages * PAGE_SIZE

    def kv_copy(page, slot, i):
        return pltpu.make_async_copy(kv_hbm.at[layer, page], kv_buf.at[slot, pl.ds(i * PAGE_SIZE, PAGE_SIZE), :],
                                     kv_sem.at[slot])

    def kr_copy(page, slot, i):
        return pltpu.make_async_copy(kr_hbm.at[layer, page], kr_buf.at[slot, :, pl.ds(i * PAGE_SIZE, PAGE_SIZE)],
                                     kr_sem.at[slot])

    def start_chunk(seq, c, slot):
        for i in range(pages):
            page = pt_ref[seq, c * pages + i]
            kv_copy(page, slot, i).start()
            kr_copy(page, slot, i).start()

    def wait_chunk(slot):
        for i in range(pages):
            kv_copy(0, slot, i).wait()
            kr_copy(0, slot, i).wait()

    @pl.when(b == 0)
    def _():
        for c in range(DECODE_AHEAD):
            start_chunk(0, c, c % DECODE_SLOTS)

    qa = qabs_ref[...]
    qp = qcat_ref[:, A_NOPE:A_NOPE + A_ROPE]
    m = jnp.full((A_HEADS, 1), -jnp.inf, F32)
    l = jnp.zeros((A_HEADS, 1), F32)
    acc = jnp.zeros((A_HEADS, KV_RANK), F32)
    for c in range(n_chunks):
        slot = c % DECODE_SLOTS
        ahead = c + DECODE_AHEAD
        if ahead < n_chunks:
            start_chunk(b, ahead, ahead % DECODE_SLOTS)
        else:
            @pl.when(b + 1 < pl.num_programs(0))
            def _():
                start_chunk(b + 1, ahead - n_chunks, ahead % DECODE_SLOTS)
        wait_chunk(slot)
        kvb = kv_buf[slot].astype(BF16)
        krb = kr_buf[slot].astype(BF16)
        s = (lax.dot_general(qa, kvb, _NT, preferred_element_type=F32)
             + jnp.dot(qp, krb, preferred_element_type=F32)) * MLA_SCALE
        m_new = jnp.maximum(m, jnp.max(s, axis=1, keepdims=True))
        alpha = jnp.exp(m - m_new)
        p = jnp.exp(s - m_new)
        l = alpha * l + jnp.sum(p, axis=1, keepdims=True)
        acc = alpha * acc + jnp.dot(p.astype(BF16), kvb, preferred_element_type=F32)
        m = m_new

    ckv_n = ckvn_ref[...]
    kpe_n = kpen_ref[...]
    s_n = (jnp.sum(qa.astype(F32) * ckv_n, axis=1, keepdims=True)
           + jnp.sum(qp.astype(F32) * kpe_n, axis=1, keepdims=True)) * MLA_SCALE
    m_f = jnp.maximum(m, s_n)
    a_past = jnp.exp(m - m_f)
    a_new = jnp.exp(s_n - m_f)
    o_ref[...] = (acc * a_past + a_new * ckv_n) / (l * a_past + a_new)


def _decode_attention(page_table, qabs, qcat, ckv_new, kpe_new, cache_kv, cache_kr_t, layer):
    batch, n_pages = page_table.shape
    pages = PAGES_PER_STEP
    n_chunks = n_pages // pages
    assert n_pages % pages == 0 and n_chunks % DECODE_SLOTS == 0 and DECODE_AHEAD < DECODE_SLOTS <= n_chunks
    qabs3 = qabs.reshape(batch, A_HEADS, KV_RANK)
    qcat3 = qcat.reshape(batch, A_HEADS, 2 * LANES)
    span = pages * PAGE_SIZE

    per_b = lambda shape: pl.BlockSpec((None,) + shape, lambda b, pt: (b, 0, 0))
    hbm = pl.BlockSpec(memory_space=pl.ANY)
    grid_spec = pltpu.PrefetchScalarGridSpec(
        num_scalar_prefetch=1,
        grid=(batch,),
        in_specs=[per_b((A_HEADS, KV_RANK)), per_b((A_HEADS, 2 * LANES)), per_b((1, KV_RANK)), per_b((1, A_ROPE)),
                  hbm, hbm],
        out_specs=per_b((A_HEADS, KV_RANK)),
        scratch_shapes=[pltpu.VMEM((DECODE_SLOTS, span, KV_RANK), F32),
                        pltpu.VMEM((DECODE_SLOTS, A_ROPE, span), F32),
                        pltpu.SemaphoreType.DMA((DECODE_SLOTS,)),
                        pltpu.SemaphoreType.DMA((DECODE_SLOTS,))],
    )
    o_lat = pl.pallas_call(
        functools.partial(_decode_kernel, layer=layer, pages=pages, n_chunks=n_chunks),
        grid_spec=grid_spec,
        out_shape=jax.ShapeDtypeStruct((batch, A_HEADS, KV_RANK), F32),
        compiler_params=_params(("arbitrary",)),
        name="mla_decode",
    )(page_table, qabs3, qcat3, ckv_new.reshape(batch, 1, KV_RANK), kpe_new.reshape(batch, 1, A_ROPE),
      cache_kv, cache_kr_t)
    return o_lat.reshape(batch, A_HEADS * KV_RANK)


def _latent_out_kernel(o_ref, wuv_ref, ya_ref):
    for h in range(A_HEADS):
        ya_ref[:, h * A_DV:(h + 1) * A_DV] = jnp.dot(
            o_ref[:, h * KV_RANK:(h + 1) * KV_RANK].astype(BF16), wuv_ref[:, h * A_DV:(h + 1) * A_DV],
            preferred_element_type=F32).astype(BF16)


def _latent_out(o_lat, wuv, layer):
    batch = o_lat.shape[0]
    return pl.pallas_call(
        _latent_out_kernel,
        grid=(1,),
        in_specs=[pl.BlockSpec(o_lat.shape, lambda i: (0, 0)),
                  pl.BlockSpec((None,) + wuv.shape[1:], lambda i: (layer, 0, 0))],
        out_specs=pl.BlockSpec((batch, A_HEADS * A_DV), lambda i: (0, 0)),
        out_shape=jax.ShapeDtypeStruct((batch, A_HEADS * A_DV), BF16),
        compiler_params=_params(("arbitrary",)),
        name="mla_latent_out",
    )(o_lat, wuv)


def _out_proj_kernel(ym_ref, ya_ref, x_ref, w1_ref, w2_ref, gpost_ref, ga_ref, gpre_ref, sc_ref, sh_ref,
                     xo_ref, h_ref):
    y = jnp.dot(ym_ref[...], w1_ref[...], preferred_element_type=F32) \
        + jnp.dot(ya_ref[...], w2_ref[...], preferred_element_type=F32)
    xn = x_ref[...] + ga_ref[...] * (_rms(y) * gpost_ref[...])
    xo_ref[...] = xn
    h_ref[...] = ((_rms(xn) * gpre_ref[...]) * (1.0 + sc_ref[...]) + sh_ref[...]).astype(BF16)


def _out_proj(ym, ya, x, w_out, g_post, ga, g_pre, sc, sh, layer):
    t, d = x.shape
    half = ym.shape[1]
    tm = min(512, t)
    vec = pl.BlockSpec((1, d), lambda i: (0, 0))
    return pl.pallas_call(
        _out_proj_kernel,
        grid=(t // tm,),
        in_specs=[
            pl.BlockSpec((tm, half), lambda i: (i, 0)),
            pl.BlockSpec((tm, half), lambda i: (i, 0)),
            pl.BlockSpec((tm, d), lambda i: (i, 0)),
            pl.BlockSpec((None, half, d), lambda i: (layer, 0, 0)),
            pl.BlockSpec((None, half, d), lambda i: (layer, 1, 0)),
            vec, _mod_spec(ga, tm, t), vec, _mod_spec(sc, tm, t), _mod_spec(sh, tm, t),
        ],
        out_specs=[pl.BlockSpec((tm, d), lambda i: (i, 0)), pl.BlockSpec((tm, d), lambda i: (i, 0))],
        out_shape=[jax.ShapeDtypeStruct((t, d), F32), jax.ShapeDtypeStruct((t, d), BF16)],
        compiler_params=_params(("arbitrary",)),
        name="out_proj",
    )(ym, ya, x, w_out, w_out, g_post, ga, g_pre, sc, sh)


def _mlp_kernel(h_ref, wu_ref, wd_ref, x_ref, gpost_ref, ga_ref, xo_ref, acc_ref):
    f = pl.program_id(1)

    @pl.when(f == 0)
    def _():
        acc_ref[...] = jnp.zeros_like(acc_ref)

    a = jnp.dot(h_ref[...], wu_ref[...], preferred_element_type=F32)
    a = jnp.square(jnp.maximum(a, 0.0)).astype(BF16)
    acc_ref[...] += jnp.dot(a, wd_ref[...], preferred_element_type=F32)

    @pl.when(f == pl.num_programs(1) - 1)
    def _():
        xo_ref[...] = x_ref[...] + ga_ref[...] * (_rms(acc_ref[...]) * gpost_ref[...])


def _mlp(h, x, w_up, w_down, g_post, ga, layer):
    t, d = x.shape
    ff = w_up.shape[2]
    tm = min(512, t)
    tf = 1024
    return pl.pallas_call(
        _mlp_kernel,
        grid=(t // tm, ff // tf),
        in_specs=[
            pl.BlockSpec((tm, d), lambda i, f: (i, 0)),
            pl.BlockSpec((None, d, tf), lambda i, f: (layer, 0, f)),
            pl.BlockSpec((None, tf, d), lambda i, f: (layer, f, 0)),
            pl.BlockSpec((tm, d), lambda i, f: (i, 0)),
            pl.BlockSpec((1, d), lambda i, f: (0, 0)),
            _mod_spec(ga, tm, t),
        ],
        out_specs=pl.BlockSpec((tm, d), lambda i, f: (i, 0)),
        out_shape=jax.ShapeDtypeStruct((t, d), F32),
        scratch_shapes=[pltpu.VMEM((tm, d), F32)],
        compiler_params=_params(("arbitrary", "arbitrary")),
        name="mlp",
    )(h, w_up, w_down, x, g_post, ga)


def _prep_w_in(w_in):
    qw, vw = M_HEADS * M_DK, M_HEADS * M_DV
    o = 2 * qw + 2 * vw
    gates = w_in[..., o:o + 2 * M_HEADS]
    o += 2 * M_HEADS
    q_lat = w_in[..., o:o + Q_RANK]
    o += Q_RANK
    kv_lat = w_in[..., o:o + KV_RANK]
    o += KV_RANK
    k_r = w_in[..., o:o + A_ROPE]
    half = A_ROPE // 2
    zeros = lambda n: jnp.zeros(w_in.shape[:-1] + (n,), w_in.dtype)
    out = jnp.concatenate([
        w_in[..., :2 * qw + 2 * vw],
        q_lat,
        gates, zeros(LANES - 2 * M_HEADS),
        kv_lat,
        k_r, k_r[..., half:], k_r[..., :half],
        zeros(U_WIDTH - U_KR - LANES),
    ], axis=-1)
    assert out.shape[-1] == U_WIDTH
    return out.astype(BF16)


def _prep_w_uq(w_uq):
    depth, r, _ = w_uq.shape
    w = w_uq.reshape(depth, r, A_HEADS, A_NOPE + A_ROPE)
    half = A_ROPE // 2
    x1 = w[..., A_NOPE:A_NOPE + half]
    x2 = w[..., A_NOPE + half:]
    out = jnp.concatenate([w[..., :A_NOPE], x1, x2, x2, x1], axis=-1)
    return out.reshape(depth, r, A_HEADS * 2 * LANES).astype(BF16)


def _rope_tables(pos):
    freqs = ROPE_THETA ** (-jnp.arange(0, A_ROPE, 2, dtype=F32) / A_ROPE)
    ang = pos[:, None] * freqs[None, :]
    cos, sin = jnp.cos(ang), jnp.sin(ang)
    z = jnp.zeros((pos.shape[0], A_ROPE), F32)
    return jnp.concatenate([cos, cos, z], axis=1), jnp.concatenate([-sin, sin, z], axis=1)


def _split_mod(ada, groups, rows):
    d = ada.shape[1] // 6
    return [ada[:, i * d:(i + 1) * d].reshape(groups, rows, d) for i in range(6)]


def kernel(x_prompt, x_sample, c_prompt, c_sample, cache_kv_latent, cache_k_rope, state_C, state_n, state_m,
           page_table, w_ada, b_ada, g_pre_mix, g_post_mix, g_pre_ff, g_post_ff, w_in, b_i, b_f, g_mh,
           g_q, w_uq, g_kv, w_uk, w_uv, w_out, w_up, w_down):
    bp, lp, d = x_prompt.shape
    bs, ls, _ = x_sample.shape
    assert ls == 1
    depth = w_in.shape[0]
    tp = bp * lp

    w_in_b = _prep_w_in(w_in)
    w_uq_b = _prep_w_uq(w_uq)
    w_uk_b = w_uk.astype(BF16)
    w_uv_b = w_uv.astype(BF16)
    w_out_b = w_out.astype(BF16)
    w_up_b = w_up.astype(BF16)
    w_down_b = w_down.astype(BF16)

    pad = (-(bp + bs)) % 8
    c_all = jnp.concatenate([c_prompt, c_sample, jnp.zeros((pad, d), F32)], axis=0)
    ada = _ada(c_all, w_ada, b_ada)

    cos_p, sin_p = _rope_tables(jnp.arange(lp, dtype=F32))
    cos_s, sin_s = _rope_tables(jnp.full((bs,), PAST_LEN, F32) + jnp.arange(ls, dtype=F32))

    cache_kr_t = jnp.swapaxes(cache_k_rope, 2, 3)

    xp = x_prompt.reshape(tp, d)
    xs = x_sample.reshape(bs, d)
    outs_p = [[] for _ in range(5)]
    outs_s = [[] for _ in range(5)]
    vec = lambda a: a.reshape(1, -1)

    for l in range(depth):
        gains = dict(g_pre=vec(g_pre_mix[l]), g_post=vec(g_post_mix[l]), g_pre_ff=vec(g_pre_ff[l]),
                     g_post_ff=vec(g_post_ff[l]), g_mh=vec(g_mh[l]), g_q=vec(g_q[l]), g_kv=vec(g_kv[l]))

        sh1, sc1, ga1, sh2, sc2, ga2 = _split_mod(ada[l, :bp], bp, 1)
        u = _in_proj(xp, gains["g_pre"], sc1, sh1, w_in_b, l)
        ym, c_new, n_new, m_new = _mlstm_prompt(u, b_i[l], b_f[l], gains["g_mh"], bp, lp)
        qcat, ckv, kpe, kn, v, kpe128 = _mla_proj(u, gains["g_q"], gains["g_kv"], w_uq_b, w_uk_b, w_uv_b,
                                                  cos_p, sin_p, l, prompt=True)
        ya = _flash(qcat, kn, kpe128, v, bp, lp)
        xp, h2 = _out_proj(ym, ya, xp, w_out_b, gains["g_post"], ga1, gains["g_pre_ff"], sc2, sh2, l)
        xp = _mlp(h2, xp, w_up_b, w_down_b, gains["g_post_ff"], ga2, l)
        for acc, val in zip(outs_p, (ckv.reshape(bp, lp, KV_RANK), kpe.reshape(bp, lp, A_ROPE),
                                     c_new, n_new, m_new[:, :, 0])):
            acc.append(val)

        sh1, sc1, ga1, sh2, sc2, ga2 = _split_mod(ada[l, bp:bp + bs], 1, bs)
        u = _in_proj(xs, gains["g_pre"], sc1, sh1, w_in_b, l)
        m0 = jnp.broadcast_to(state_m[l][:, :, None], (bs, M_HEADS, LANES))
        ym, c_new, n_new, m_new = _mlstm_sample(u, b_i[l], b_f[l], gains["g_mh"], state_C, state_n, m0, l)
        qcat, ckv, kpe, qabs = _mla_proj(u, gains["g_q"], gains["g_kv"], w_uq_b, w_uk_b, None,
                                         cos_s, sin_s, l, prompt=False)
        o_lat = _decode_attention(page_table, qabs, qcat, ckv, kpe, cache_kv_latent, cache_kr_t, l)
        ya = _latent_out(o_lat, w_uv_b, l)
        xs, h2 = _out_proj(ym, ya, xs, w_out_b, gains["g_post"], ga1, gains["g_pre_ff"], sc2, sh2, l)
        xs = _mlp(h2, xs, w_up_b, w_down_b, gains["g_post_ff"], ga2, l)
        for acc, val in zip(outs_s, (ckv.reshape(bs, ls, KV_RANK), kpe.reshape(bs, ls, A_ROPE),
                                     c_new, n_new, m_new[:, :, 0])):
            acc.append(val)

    return (xp.reshape(bp, lp, d), xs.reshape(bs, ls, d),
            *[jnp.stack(a) for a in outs_p], *[jnp.stack(a) for a in outs_s])
```

```python
import functools

import jax
import jax.numpy as jnp
import numpy as np
from jax import lax
from jax.experimental import pallas as pl
from jax.experimental.pallas import tpu as pltpu

F32 = jnp.float32
BF16 = jnp.bfloat16

EPS = 1e-6
M_HEADS = 4
M_DK = 128
M_DV = 256
M_CHUNK = 128
A_HEADS = 8
A_DV = 128
A_NOPE = 128
A_ROPE = 64
Q_RANK = 384
KV_RANK = 256
ROPE_THETA = 10000.0
PAST_LEN = 16384
PAGE_SIZE = 128
LANES = 128

U_WIDTH = 4096
U_Q, U_K, U_V, U_O = 0, 512, 1024, 2048
U_QLAT = 3072
U_GATES = 3456
U_KVLAT = 3584
U_KR = 3840

V7X_VMEM_LIMIT = 56 * 1024 * 1024
MLA_SCALE = float((A_NOPE + A_ROPE) ** -0.5)
MLSTM_SCALE = float(M_DK ** -0.5)
PAGES_PER_STEP = 16
DECODE_SLOTS = 4
DECODE_AHEAD = 3
STEP_BATCH = 8
CHUNK_SEQS = 4

_NT = (((1,), (1,)), ((), ()))


def _params(semantics, vmem=V7X_VMEM_LIMIT):
    return pltpu.CompilerParams(dimension_semantics=semantics, vmem_limit_bytes=vmem)


def _rms(x):
    return x * lax.rsqrt(jnp.mean(x * x, axis=-1, keepdims=True) + EPS)


def _log_sigmoid(x):
    return jnp.minimum(x, 0.0) - jnp.log1p(jnp.exp(-jnp.abs(x)))


def _swap_halves(x):
    return jnp.concatenate([x[:, 64:], x[:, :64]], axis=1)


def _ada_kernel(c_ref, w_ref, b_ref, o_ref):
    c = c_ref[...]
    a = (c * jax.nn.sigmoid(c)).astype(BF16)
    o_ref[...] = jnp.dot(a, w_ref[...].astype(BF16), preferred_element_type=F32) + b_ref[...]


def _ada(c_all, w_ada, b_ada):
    depth, d, n = w_ada.shape
    r = c_all.shape[0]
    tn = 1024
    return pl.pallas_call(
        _ada_kernel,
        grid=(depth, n // tn),
        in_specs=[
            pl.BlockSpec((r, d), lambda l, j: (0, 0)),
            pl.BlockSpec((None, d, tn), lambda l, j: (l, 0, j)),
            pl.BlockSpec((None, 1, tn), lambda l, j: (l, 0, j)),
        ],
        out_specs=pl.BlockSpec((None, r, tn), lambda l, j: (l, 0, j)),
        out_shape=jax.ShapeDtypeStruct((depth, r, n), F32),
        compiler_params=_params(("arbitrary", "arbitrary")),
        name="ada",
    )(c_all, w_ada, b_ada.reshape(depth, 1, n))


def _in_proj_kernel(x_ref, g_ref, sc_ref, sh_ref, w_ref, o_ref, h_ref):
    @pl.when(pl.program_id(1) == 0)
    def _():
        h = (_rms(x_ref[...]) * g_ref[...]) * (1.0 + sc_ref[...]) + sh_ref[...]
        h_ref[...] = h.astype(BF16)

    o_ref[...] = jnp.dot(h_ref[...], w_ref[...], preferred_element_type=F32)


def _mod_spec(mod, tm, t):
    g, r, d = mod.shape
    rows_per_group = t // g
    assert r in (1, tm) and rows_per_group % tm == 0
    return pl.BlockSpec((None, r, d), lambda i, *_: ((i * tm) // rows_per_group, 0, 0))


def _in_proj_ahead_kernel(x0_ref, xq_ref, g_ref, sc0_ref, sh0_ref, scn_ref, shn_ref, w_ref, o_ref, ha_ref, hb_ref,
                          *, part):
    i = pl.program_id(0)
    j = pl.program_id(1)

    def modulate(x, sc, sh):
        return ((_rms(x) * g_ref[...]) * (1.0 + sc) + sh).astype(BF16)

    @pl.when(jnp.logical_and(i == 0, j == 0))
    def _():
        ha_ref[...] = modulate(x0_ref[...], sc0_ref[...], sh0_ref[...])

    rows = pl.ds(pl.multiple_of(j * part, part), part)

    def step(cur_ref, nxt_ref):
        nxt_ref[rows, :] = modulate(xq_ref[...], scn_ref[...], shn_ref[...])
        o_ref[...] = jnp.dot(cur_ref[...], w_ref[...], preferred_element_type=F32)

    @pl.when(i % 2 == 0)
    def _():
        step(ha_ref, hb_ref)

    @pl.when(i % 2 == 1)
    def _():
        step(hb_ref, ha_ref)


def _in_proj_ahead(x, g, sc, sh, w, layer, tm, tn):
    t, d = x.shape
    n = w.shape[2]
    nt, nj = t // tm, n // tn
    part = tm // nj
    groups = sc.shape[0]
    rows_per_group = t // groups
    assert sc.shape[1] == 1 and rows_per_group % tm == 0 and tm % nj == 0
    ahead = lambda i: jnp.minimum(i + 1, nt - 1)
    mod0 = pl.BlockSpec((None, 1, d), lambda i, j: (0, 0, 0))
    modn = pl.BlockSpec((None, 1, d), lambda i, j: ((ahead(i) * tm) // rows_per_group, 0, 0))
    return pl.pallas_call(
        functools.partial(_in_proj_ahead_kernel, part=part),
        grid=(nt, nj),
        in_specs=[
            pl.BlockSpec((tm, d), lambda i, j: (0, 0)),
            pl.BlockSpec((part, d), lambda i, j: (ahead(i) * nj + j, 0)),
            pl.BlockSpec((1, d), lambda i, j: (0, 0)),
            mod0, mod0, modn, modn,
            pl.BlockSpec((None, d, tn), lambda i, j: (layer, 0, j)),
        ],
        out_specs=pl.BlockSpec((tm, tn), lambda i, j: (i, j)),
        out_shape=jax.ShapeDtypeStruct((t, n), F32),
        scratch_shapes=[pltpu.VMEM((tm, d), BF16), pltpu.VMEM((tm, d), BF16)],
        compiler_params=_params(("arbitrary", "arbitrary")),
        name="in_proj",
    )(x, x, g, sc, sh, sc, sh, w)


def _in_proj(x, g, sc, sh, w, layer):
    t, d = x.shape
    n = w.shape[2]
    tm = min(1024, t)
    tn = 1024
    if t // tm > 1 and sc.shape[1] == 1:
        return _in_proj_ahead(x, g, sc, sh, w, layer, tm, tn)
    return pl.pallas_call(
        _in_proj_kernel,
        grid=(t // tm, n // tn),
        in_specs=[
            pl.BlockSpec((tm, d), lambda i, j: (i, 0)),
            pl.BlockSpec((1, d), lambda i, j: (0, 0)),
            _mod_spec(sc, tm, t),
            _mod_spec(sh, tm, t),
            pl.BlockSpec((None, d, tn), lambda i, j: (layer, 0, j)),
        ],
        out_specs=pl.BlockSpec((tm, tn), lambda i, j: (i, j)),
        out_shape=jax.ShapeDtypeStruct((t, n), F32),
        scratch_shapes=[pltpu.VMEM((tm, d), BF16)],
        compiler_params=_params(("arbitrary", "arbitrary")),
        name="in_proj",
    )(x, g, sc, sh, w)


def _head_out(hh, gmh, o_pre):
    return ((_rms(hh) * gmh) * jax.nn.sigmoid(o_pre)).astype(BF16)


def _mlstm_chunk_kernel(bi_ref, bf_ref, q_ref, k_ref, v_ref, o_ref, g_ref, gmh_ref,
                        ym_ref, cout_ref, nout_ref, mout_ref, ct_ref, n_ref, m_ref, *, seqs):
    c = pl.program_id(1)
    cs = M_CHUNK

    @pl.when(c == 0)
    def _():
        ct_ref[...] = jnp.zeros_like(ct_ref)
        n_ref[...] = jnp.zeros_like(n_ref)
        m_ref[...] = jnp.zeros_like(m_ref)

    lane = lax.broadcasted_iota(jnp.int32, (1, LANES), 1)
    bias = jnp.zeros((1, LANES), F32)
    for h in range(M_HEADS):
        bias = jnp.where(lane == h, bi_ref[h], bias)
        bias = jnp.where(lane == M_HEADS + h, bf_ref[h], bias)
    row = lax.broadcasted_iota(jnp.int32, (cs, cs), 0)
    col = lax.broadcasted_iota(jnp.int32, (cs, cs), 1)
    causal = col <= row

    def gates(sq):
        g = g_ref[sq] + bias
        gt = g.T[:2 * M_HEADS, :]
        return g, gt, _log_sigmoid(g), _log_sigmoid(gt)

    def chain(sq, h, g, gt, ls, lst):
        st = sq * M_HEADS + h
        i_col = g[:, h:h + 1]
        f_col = ls[:, M_HEADS + h:M_HEADS + h + 1]
        i_row = gt[h:h + 1, :]
        f_row = lst[M_HEADS + h:M_HEADS + h + 1, :]
        q = q_ref[sq, :, h * M_DK:(h + 1) * M_DK]
        k = k_ref[sq, :, h * M_DK:(h + 1) * M_DK]
        v = v_ref[sq, :, h * M_DV:(h + 1) * M_DV]
        ct = ct_ref[st]
        n = n_ref[st]
        m_prev = m_ref[st][:, :1]
        qb = q.astype(BF16)
        ktb = k.T.astype(BF16)
        b_col = jnp.sum(jnp.where(causal, f_row, 0.0), axis=1, keepdims=True)
        b_row = jnp.sum(jnp.where(row <= col, f_col, 0.0), axis=0, keepdims=True)
        qk = jnp.dot(qb, ktb, preferred_element_type=F32)
        inter = jnp.dot(qb, ct.astype(BF16), preferred_element_type=F32)
        qn = jnp.sum(q * n, axis=1, keepdims=True)
        yield
        dm = jnp.where(causal, b_col - b_row + i_row, -jnp.inf)
        log_inter = b_col + m_prev
        dm_max = jnp.max(dm, axis=1, keepdims=True)
        yield
        m_t = jnp.maximum(log_inter, dm_max)
        w_inter = jnp.exp(log_inter - m_t) * MLSTM_SCALE
        s = qk * (jnp.exp(dm - m_t) * MLSTM_SCALE)
        sb = s.astype(BF16)
        s_sum = jnp.dot(sb, jnp.ones((cs, LANES), BF16), preferred_element_type=F32)
        num = w_inter * inter + jnp.dot(sb, v.astype(BF16), preferred_element_type=F32)
        m_new = m_t[cs - 1:cs, :]
        b_last = b_col[cs - 1:cs, :]
        a_inter = jnp.exp(b_last + m_prev - m_new)
        a_col = jnp.exp(b_last - b_col + i_col - m_new)
        ct_new = a_inter * ct + jnp.dot(ktb, (v * a_col).astype(BF16), preferred_element_type=F32)
        n_new = a_inter * n + jnp.sum(a_col * k, axis=0, keepdims=True)
        yield
        den = jnp.maximum(jnp.abs(w_inter * qn + s_sum), jnp.exp(-m_t))
        hh = num / jnp.concatenate([den, den], axis=1)
        ms = jnp.dot((hh * hh).astype(BF16), jnp.full((M_DV, LANES), 1.0 / M_DV, BF16),
                     preferred_element_type=F32)
        ms = jnp.concatenate([ms, ms], axis=1)
        yield
        sl = slice(h * M_DV, (h + 1) * M_DV)
        out = ((hh * lax.rsqrt(ms + EPS)) * gmh_ref[:, sl]) * jax.nn.sigmoid(o_ref[sq, :, sl])
        ct_ref[st] = ct_new
        n_ref[st] = n_new
        m_ref[st] = jnp.broadcast_to(m_new, (1, LANES))
        ym_ref[sq, :, sl] = out.astype(BF16)

    chains = []
    for sq in range(seqs):
        gate_vals = gates(sq)
        chains += [chain(sq, h, *gate_vals) for h in range(M_HEADS)]
    for _ in range(4):
        for ch in chains:
            next(ch)
    for ch in chains:
        assert next(ch, None) is None

    @pl.when(c == pl.num_programs(1) - 1)
    def _():
        for sq in range(seqs):
            for h in range(M_HEADS):
                st = sq * M_HEADS + h
                cout_ref[sq, h] = ct_ref[st].T
                nout_ref[sq, h:h + 1, :] = n_ref[st]
                mout_ref[sq, h:h + 1, :] = m_ref[st]


def _mlstm_prompt(u, b_i, b_f, g_mh, batch, seq):
    t = u.shape[0]
    nc = seq // M_CHUNK
    cs = M_CHUNK
    sb = min(CHUNK_SEQS, batch)
    assert batch % sb == 0
    qw, vw = M_HEADS * M_DK, M_HEADS * M_DV
    u3 = u.reshape(batch, seq, U_WIDTH)
    smem = pl.BlockSpec(memory_space=pltpu.SMEM)
    ym, c_new, n_new, m_new = pl.pallas_call(
        functools.partial(_mlstm_chunk_kernel, seqs=sb),
        grid=(batch // sb, nc),
        in_specs=[
            smem, smem,
            pl.BlockSpec((sb, cs, qw), lambda b, c: (b, c, U_Q // qw)),
            pl.BlockSpec((sb, cs, qw), lambda b, c: (b, c, U_K // qw)),
            pl.BlockSpec((sb, cs, vw), lambda b, c: (b, c, U_V // vw)),
            pl.BlockSpec((sb, cs, vw), lambda b, c: (b, c, U_O // vw)),
            pl.BlockSpec((sb, cs, LANES), lambda b, c: (b, c, U_GATES // LANES)),
            pl.BlockSpec((1, vw), lambda b, c: (0, 0)),
        ],
        out_specs=[
            pl.BlockSpec((sb, cs, vw), lambda b, c: (b, c, 0)),
            pl.BlockSpec((sb, M_HEADS, M_DV, M_DK), lambda b, c: (b, 0, 0, 0)),
            pl.BlockSpec((sb, M_HEADS, M_DK), lambda b, c: (b, 0, 0)),
            pl.BlockSpec((sb, M_HEADS, LANES), lambda b, c: (b, 0, 0)),
        ],
        out_shape=[
            jax.ShapeDtypeStruct((batch, seq, vw), BF16),
            jax.ShapeDtypeStruct((batch, M_HEADS, M_DV, M_DK), F32),
            jax.ShapeDtypeStruct((batch, M_HEADS, M_DK), F32),
            jax.ShapeDtypeStruct((batch, M_HEADS, LANES), F32),
        ],
        scratch_shapes=[
            pltpu.VMEM((sb * M_HEADS, M_DK, M_DV), F32),
            pltpu.VMEM((sb * M_HEADS, 1, M_DK), F32),
            pltpu.VMEM((sb * M_HEADS, 1, LANES), F32),
        ],
        compiler_params=_params(("arbitrary", "arbitrary")),
        name="mlstm_prompt",
    )(b_i, b_f, u3, u3, u3, u3, u3, g_mh)
    return ym.reshape(t, vw), c_new, n_new, m_new


def _mlstm_step_kernel(bi_ref, bf_ref, q_ref, k_ref, v_ref, o_ref, g_ref, gmh_ref, c0_ref, n0_ref, m0_ref,
                       ym_ref, cout_ref, nout_ref, mout_ref):
    eye = jnp.where(lax.broadcasted_iota(jnp.int32, (M_DV, M_DV), 0)
                    == lax.broadcasted_iota(jnp.int32, (M_DV, M_DV), 1), 1.0, 0.0).astype(BF16)
    for b, h in [(b, h) for b in range(STEP_BATCH) for h in range(M_HEADS)]:
        g = g_ref[b]
        ig = g[:, h:h + 1] + bi_ref[h]
        lf = _log_sigmoid(g[:, M_HEADS + h:M_HEADS + h + 1] + bf_ref[h])
        m0 = m0_ref[b, h:h + 1, :1]
        log_inter = lf + m0
        m_t = jnp.maximum(log_inter, ig)
        w_inter = jnp.exp(log_inter - m_t)
        p = jnp.exp(ig - m_t)

        q = q_ref[b, :, h * M_DK:(h + 1) * M_DK]
        k = k_ref[b, :, h * M_DK:(h + 1) * M_DK]
        v = v_ref[b, :, h * M_DV:(h + 1) * M_DV]
        c0 = c0_ref[b, h]
        n0 = n0_ref[b, h:h + 1, :]

        s = jnp.sum(q * k, axis=1, keepdims=True) * (p * MLSTM_SCALE)
        q8 = jnp.broadcast_to(q, (8, M_DK)).astype(BF16)
        cq = lax.dot_general(q8, c0.astype(BF16), _NT, preferred_element_type=F32)[:1, :]
        wq = w_inter * MLSTM_SCALE
        num = wq * cq + s * v
        den = wq * jnp.sum(n0 * q, axis=1, keepdims=True) + s
        hh = num / jnp.maximum(jnp.abs(den), jnp.exp(-m_t))

        pv = jnp.broadcast_to(p * v, (LANES, M_DV)).astype(BF16)
        pv_col = lax.dot_general(eye, pv, _NT, preferred_element_type=F32)
        cout_ref[b, h] = w_inter * c0 + pv_col * k
        nout_ref[b, h:h + 1, :] = w_inter * n0 + p * k
        mout_ref[b, h:h + 1, :] = jnp.broadcast_to(m_t, (1, LANES))

        sl = slice(h * M_DV, (h + 1) * M_DV)
        ym_ref[b, :, sl] = _head_out(hh, gmh_ref[:, sl], o_ref[b, :, sl])


def _mlstm_sample(u, b_i, b_f, g_mh, c0, n0, m0, layer):
    batch = u.shape[0]
    sb = STEP_BATCH
    assert batch % sb == 0
    qw, vw = M_HEADS * M_DK, M_HEADS * M_DV
    u3 = u.reshape(batch, 1, U_WIDTH)
    smem = pl.BlockSpec(memory_space=pltpu.SMEM)
    ym, c, n, m = pl.pallas_call(
        _mlstm_step_kernel,
        grid=(batch // sb,),
        in_specs=[
            smem, smem,
            pl.BlockSpec((sb, 1, qw), lambda b: (b, 0, U_Q // qw)),
            pl.BlockSpec((sb, 1, qw), lambda b: (b, 0, U_K // qw)),
            pl.BlockSpec((sb, 1, vw), lambda b: (b, 0, U_V // vw)),
            pl.BlockSpec((sb, 1, vw), lambda b: (b, 0, U_O // vw)),
            pl.BlockSpec((sb, 1, LANES), lambda b: (b, 0, U_GATES // LANES)),
            pl.BlockSpec((1, vw), lambda b: (0, 0)),
            pl.BlockSpec((None, sb, M_HEADS, M_DV, M_DK), lambda b: (layer, b, 0, 0, 0)),
            pl.BlockSpec((None, sb, M_HEADS, M_DK), lambda b: (layer, b, 0, 0)),
            pl.BlockSpec((sb, M_HEADS, LANES), lambda b: (b, 0, 0)),
        ],
        out_specs=[
            pl.BlockSpec((sb, 1, vw), lambda b: (b, 0, 0)),
            pl.BlockSpec((sb, M_HEADS, M_DV, M_DK), lambda b: (b, 0, 0, 0)),
            pl.BlockSpec((sb, M_HEADS, M_DK), lambda b: (b, 0, 0)),
            pl.BlockSpec((sb, M_HEADS, LANES), lambda b: (b, 0, 0)),
        ],
        out_shape=[
            jax.ShapeDtypeStruct((batch, 1, vw), BF16),
            jax.ShapeDtypeStruct((batch, M_HEADS, M_DV, M_DK), F32),
            jax.ShapeDtypeStruct((batch, M_HEADS, M_DK), F32),
            jax.ShapeDtypeStruct((batch, M_HEADS, LANES), F32),
        ],
        compiler_params=_params(("arbitrary",)),
        name="mlstm_sample",
    )(b_i, b_f, u3, u3, u3, u3, u3, g_mh, c0, n0, m0)
    return ym.reshape(batch, vw), c, n, m


def _mla_common(ql_ref, kvl_ref, kr_ref, gq_ref, gkv_ref, wq_ref, cos_ref, sin_ref,
                qcat_ref, ckv_ref, kpe_ref):
    cos = cos_ref[...]
    sin = sin_ref[...]
    hq = (_rms(ql_ref[...]) * gq_ref[...]).astype(BF16)
    q = jnp.dot(hq, wq_ref[...], preferred_element_type=F32)
    for h in range(A_HEADS):
        base = h * 2 * LANES
        qr = q[:, base + LANES:base + 2 * LANES]
        qcat_ref[:, base:base + LANES] = q[:, base:base + LANES].astype(BF16)
        qcat_ref[:, base + LANES:base + 2 * LANES] = (qr * cos + _swap_halves(qr) * sin).astype(BF16)
    ckv = _rms(kvl_ref[...]) * gkv_ref[...]
    ckv_ref[...] = ckv
    kr = kr_ref[...]
    kpe128 = kr * cos + _swap_halves(kr) * sin
    kpe_ref[...] = kpe128[:, :A_ROPE]
    return q, ckv, kpe128


def _mla_proj_prompt_kernel(ql_ref, kvl_ref, kr_ref, gq_ref, gkv_ref, wq_ref, wuk_ref, wuv_ref, cos_ref, sin_ref,
                            qcat_ref, ckv_ref, kpe_ref, kn_ref, v_ref, kpe128_ref):
    _, ckv, kpe128 = _mla_common(ql_ref, kvl_ref, kr_ref, gq_ref, gkv_ref, wq_ref, cos_ref, sin_ref,
                                 qcat_ref, ckv_ref, kpe_ref)
    cb = ckv.astype(BF16)
    kn_ref[...] = jnp.dot(cb, wuk_ref[...], preferred_element_type=F32).astype(BF16)
    v_ref[...] = jnp.dot(cb, wuv_ref[...], preferred_element_type=F32).astype(BF16)
    kpe128_ref[...] = kpe128.astype(BF16)


def _mla_proj_sample_kernel(ql_ref, kvl_ref, kr_ref, gq_ref, gkv_ref, wq_ref, wuk_ref, cos_ref, sin_ref,
                            qcat_ref, ckv_ref, kpe_ref, qabs_ref):
    q, _, _ = _mla_common(ql_ref, kvl_ref, kr_ref, gq_ref, gkv_ref, wq_ref, cos_ref, sin_ref,
                          qcat_ref, ckv_ref, kpe_ref)
    for h in range(A_HEADS):
        qn = q[:, h * 2 * LANES:h * 2 * LANES + A_NOPE].astype(BF16)
        wk = wuk_ref[:, h * A_NOPE:(h + 1) * A_NOPE]
        qabs_ref[:, h * KV_RANK:(h + 1) * KV_RANK] = lax.dot_general(
            qn, wk, _NT, preferred_element_type=F32).astype(BF16)


def _mla_proj(u, g_q, g_kv, wq, wuk, wuv, cos, sin, layer, *, prompt):
    t = u.shape[0]
    tm = min(512, t)
    npos = cos.shape[0] // tm
    hw = A_HEADS * 2 * LANES
    full = lambda a: pl.BlockSpec(a.shape, lambda i: (0, 0))
    of_layer = lambda a: pl.BlockSpec((None,) + a.shape[1:], lambda i: (layer, 0, 0))
    in_specs = [
        pl.BlockSpec((tm, Q_RANK), lambda i: (i, U_QLAT // Q_RANK)),
        pl.BlockSpec((tm, KV_RANK), lambda i: (i, U_KVLAT // KV_RANK)),
        pl.BlockSpec((tm, LANES), lambda i: (i, U_KR // LANES)),
        full(g_q), full(g_kv), of_layer(wq), of_layer(wuk),
    ]
    args = [u, u, u, g_q, g_kv, wq, wuk]
    if prompt:
        in_specs.append(of_layer(wuv))
        args.append(wuv)
    in_specs += [pl.BlockSpec((tm, LANES), lambda i: (i % npos, 0))] * 2
    args += [cos, sin]
    row = lambda w: pl.BlockSpec((tm, w), lambda i: (i, 0))
    out_specs = [row(hw), row(KV_RANK), row(A_ROPE)]
    out_shape = [jax.ShapeDtypeStruct((t, hw), BF16), jax.ShapeDtypeStruct((t, KV_RANK), F32),
                 jax.ShapeDtypeStruct((t, A_ROPE), F32)]
    if prompt:
        out_specs += [row(A_HEADS * A_NOPE), row(A_HEADS * A_DV), row(LANES)]
        out_shape += [jax.ShapeDtypeStruct((t, A_HEADS * A_NOPE), BF16),
                      jax.ShapeDtypeStruct((t, A_HEADS * A_DV), BF16),
                      jax.ShapeDtypeStruct((t, LANES), BF16)]
        body = _mla_proj_prompt_kernel
    else:
        out_specs += [row(A_HEADS * KV_RANK)]
        out_shape += [jax.ShapeDtypeStruct((t, A_HEADS * KV_RANK), BF16)]
        body = _mla_proj_sample_kernel
    return pl.pallas_call(
        body,
        grid=(t // tm,),
        in_specs=in_specs,
        out_specs=out_specs,
        out_shape=out_shape,
        compiler_params=_params(("arbitrary",)),
        name="mla_proj_prompt" if prompt else "mla_proj_sample",
    )(*args)


FLASH_HEADS = 8
EXP2_SCALE = float(MLA_SCALE * np.log2(np.e))


def _flash_kernel(q_ref, kn_ref, kpe_ref, v_ref, o_ref, m_ref, acc_ref, *, tile):
    qi = pl.program_id(2)
    m_ref[...] = jnp.full_like(m_ref, -jnp.inf)
    acc_ref[...] = jnp.zeros_like(acc_ref)
    ones = jnp.ones((tile, LANES), BF16)
    reps = tile // LANES

    def step(ki, diagonal):
        off = pl.multiple_of(ki * tile, tile)
        kpe = kpe_ref[pl.ds(off, tile), :]
        if diagonal:
            row = lax.broadcasted_iota(jnp.int32, (tile, tile), 0)
            col = lax.broadcasted_iota(jnp.int32, (tile, tile), 1)
            keep = col <= row
        for h in range(FLASH_HEADS):
            q = q_ref[:, h * 2 * LANES:(h + 1) * 2 * LANES]
            k = jnp.concatenate([kn_ref[pl.ds(off, tile), h * A_NOPE:(h + 1) * A_NOPE], kpe], axis=1)
            v = jnp.concatenate([v_ref[pl.ds(off, tile), h * A_DV:(h + 1) * A_DV], ones], axis=1)
            s = lax.dot_general(q, k, _NT, preferred_element_type=F32)
            if diagonal:
                s = jnp.where(keep, s, -jnp.inf)
            m_prev = m_ref[h]
            m_new = jnp.maximum(m_prev, jnp.max(s, axis=1, keepdims=True))
            alpha = jnp.exp2((m_prev - m_new) * EXP2_SCALE)
            p = jnp.exp2((s - jnp.concatenate([m_new] * reps, axis=1)) * EXP2_SCALE)
            acc_ref[h] = jnp.concatenate([alpha, alpha], axis=1) * acc_ref[h] \
                + jnp.dot(p.astype(BF16), v, preferred_element_type=F32)
            m_ref[h] = m_new

    def body(ki, carry):
        step(ki, False)
        return carry

    lax.fori_loop(0, qi, body, 0)
    step(qi, True)
    for h in range(FLASH_HEADS):
        acc = acc_ref[h]
        o_ref[:, h * A_DV:(h + 1) * A_DV] = (acc[:, :A_DV] / acc[:, A_DV:]).astype(BF16)


def _flash(qcat, kn, kpe128, v, batch, seq):
    t = qcat.shape[0]
    tile = min(512, seq)
    nq = seq // tile
    hp = FLASH_HEADS
    return pl.pallas_call(
        functools.partial(_flash_kernel, tile=tile),
        grid=(batch, A_HEADS // hp, nq),
        in_specs=[
            pl.BlockSpec((tile, hp * 2 * LANES), lambda b, h, i: (b * nq + i, h)),
            pl.BlockSpec((seq, hp * A_NOPE), lambda b, h, i: (b, h)),
            pl.BlockSpec((seq, LANES), lambda b, h, i: (b, 0)),
            pl.BlockSpec((seq, hp * A_DV), lambda b, h, i: (b, h)),
        ],
        out_specs=pl.BlockSpec((tile, hp * A_DV), lambda b, h, i: (b * nq + i, h)),
        out_shape=jax.ShapeDtypeStruct((t, A_HEADS * A_DV), BF16),
        scratch_shapes=[pltpu.VMEM((hp, tile, LANES), F32), pltpu.VMEM((hp, tile, 2 * A_DV), F32)],
        compiler_params=_params(("arbitrary", "arbitrary", "arbitrary")),
        name="mla_flash",
    )(qcat, kn, kpe128, v)


def _decode_kernel(pt_ref, qabs_ref, qcat_ref, ckvn_ref, kpen_ref, kv_hbm, kr_hbm, o_ref,
                   kv_buf, kr_buf, kv_sem, kr_sem, *, layer, pages, n_chunks):
    b = pl.program_id(0)
    span = pages * PAGE_SIZE

    def kv_copy(page, slot, i):
        return pltpu.make_async_copy(kv_hbm.at[layer, page], kv_buf.at[slot, pl.ds(i * PAGE_SIZE, PAGE_SIZE), :],
                                     kv_sem.at[slot])

    def kr_copy(page, slot, i):
        return pltpu.make_async_copy(kr_hbm.at[layer, page], kr_buf.at[slot, :, pl.ds(i * PAGE_SIZE, PAGE_SIZE)],
                                     kr_sem.at[slot])

    def start_chunk(seq, c, slot):
        for i in range(pages):
            page = pt_ref[seq, c * pages + i]
            kv_copy(page, slot, i).start()
            kr_copy(page, slot, i).start()

    def wait_chunk(slot):
        for i in range(pages):
            kv_copy(0, slot, i).wait()
            kr_copy(0, slot, i).wait()

    @pl.when(b == 0)
    def _():
        for c in range(DECODE_AHEAD):
            start_chunk(0, c, c % DECODE_SLOTS)

    qa = qabs_ref[...]
    qp = qcat_ref[:, A_NOPE:A_NOPE + A_ROPE]
    m = jnp.full((A_HEADS, 1), -jnp.inf, F32)
    l = jnp.zeros((A_HEADS, 1), F32)
    acc = jnp.zeros((A_HEADS, KV_RANK), F32)
    for c in range(n_chunks):
        slot = c % DECODE_SLOTS
        ahead = c + DECODE_AHEAD
        if ahead < n_chunks:
            start_chunk(b, ahead, ahead % DECODE_SLOTS)
        else:
            @pl.when(b + 1 < pl.num_programs(0))
            def _():
                start_chunk(b + 1, ahead - n_chunks, ahead % DECODE_SLOTS)
        wait_chunk(slot)
        kvb = kv_buf[slot].astype(BF16)
        krb = kr_buf[slot].astype(BF16)
        s = (lax.dot_general(qa, kvb, _NT, preferred_element_type=F32)
             + jnp.dot(qp, krb, preferred_element_type=F32)) * MLA_SCALE
        m_new = jnp.maximum(m, jnp.max(s, axis=1, keepdims=True))
        alpha = jnp.exp(m - m_new)
        p = jnp.exp(s - m_new)
        l = alpha * l + jnp.sum(p, axis=1, keepdims=True)
        acc = alpha * acc + jnp.dot(p.astype(BF16), kvb, preferred_element_type=F32)
        m = m_new

    ckv_n = ckvn_ref[...]
    kpe_n = kpen_ref[...]
    s_n = (jnp.sum(qa.astype(F32) * ckv_n, axis=1, keepdims=True)
           + jnp.sum(qp.astype(F32) * kpe_n, axis=1, keepdims=True)) * MLA_SCALE
    m_f = jnp.maximum(m, s_n)
    a_past = jnp.exp(m - m_f)
    a_new = jnp.exp(s_n - m_f)
    o_ref[...] = (acc * a_past + a_new * ckv_n) / (l * a_past + a_new)


def _decode_attention(page_table, qabs, qcat, ckv_new, kpe_new, cache_kv, cache_kr_t, layer):
    batch, n_pages = page_table.shape
    pages = PAGES_PER_STEP
    n_chunks = n_pages // pages
    assert n_pages % pages == 0 and n_chunks % DECODE_SLOTS == 0 and DECODE_AHEAD < DECODE_SLOTS <= n_chunks
    qabs3 = qabs.reshape(batch, A_HEADS, KV_RANK)
    qcat3 = qcat.reshape(batch, A_HEADS, 2 * LANES)
    span = pages * PAGE_SIZE

    per_b = lambda shape: pl.BlockSpec((None,) + shape, lambda b, pt: (b, 0, 0))
    hbm = pl.BlockSpec(memory_space=pl.ANY)
    grid_spec = pltpu.PrefetchScalarGridSpec(
        num_scalar_prefetch=1,
        grid=(batch,),
        in_specs=[per_b((A_HEADS, KV_RANK)), per_b((A_HEADS, 2 * LANES)), per_b((1, KV_RANK)), per_b((1, A_ROPE)),
                  hbm, hbm],
        out_specs=per_b((A_HEADS, KV_RANK)),
        scratch_shapes=[pltpu.VMEM((DECODE_SLOTS, span, KV_RANK), F32),
                        pltpu.VMEM((DECODE_SLOTS, A_ROPE, span), F32),
                        pltpu.SemaphoreType.DMA((DECODE_SLOTS,)),
                        pltpu.SemaphoreType.DMA((DECODE_SLOTS,))],
    )
    o_lat = pl.pallas_call(
        functools.partial(_decode_kernel, layer=layer, pages=pages, n_chunks=n_chunks),
        grid_spec=grid_spec,
        out_shape=jax.ShapeDtypeStruct((batch, A_HEADS, KV_RANK), F32),
        compiler_params=_params(("arbitrary",)),
        name="mla_decode",
    )(page_table, qabs3, qcat3, ckv_new.reshape(batch, 1, KV_RANK), kpe_new.reshape(batch, 1, A_ROPE),
      cache_kv, cache_kr_t)
    return o_lat.reshape(batch, A_HEADS * KV_RANK)


def _latent_out_kernel(o_ref, wuv_ref, ya_ref):
    for h in range(A_HEADS):
        ya_ref[:, h * A_DV:(h + 1) * A_DV] = jnp.dot(
            o_ref[:, h * KV_RANK:(h + 1) * KV_RANK].astype(BF16), wuv_ref[:, h * A_DV:(h + 1) * A_DV],
            preferred_element_type=F32).astype(BF16)


def _latent_out(o_lat, wuv, layer):
    batch = o_lat.shape[0]
    return pl.pallas_call(
        _latent_out_kernel,
        grid=(1,),
        in_specs=[pl.BlockSpec(o_lat.shape, lambda i: (0, 0)),
                  pl.BlockSpec((None,) + wuv.shape[1:], lambda i: (layer, 0, 0))],
        out_specs=pl.BlockSpec((batch, A_HEADS * A_DV), lambda i: (0, 0)),
        out_shape=jax.ShapeDtypeStruct((batch, A_HEADS * A_DV), BF16),
        compiler_params=_params(("arbitrary",)),
        name="mla_latent_out",
    )(o_lat, wuv)


def _out_proj_kernel(ym_ref, ya_ref, x_ref, w1_ref, w2_ref, gpost_ref, ga_ref, gpre_ref, sc_ref, sh_ref,
                     xo_ref, h_ref):
    y = jnp.dot(ym_ref[...], w1_ref[...], preferred_element_type=F32) \
        + jnp.dot(ya_ref[...], w2_ref[...], preferred_element_type=F32)
    xn = x_ref[...] + ga_ref[...] * (_rms(y) * gpost_ref[...])
    xo_ref[...] = xn
    h_ref[...] = ((_rms(xn) * gpre_ref[...]) * (1.0 + sc_ref[...]) + sh_ref[...]).astype(BF16)


def _out_proj(ym, ya, x, w_out, g_post, ga, g_pre, sc, sh, layer):
    t, d = x.shape
    half = ym.shape[1]
    tm = min(512, t)
    vec = pl.BlockSpec((1, d), lambda i: (0, 0))
    return pl.pallas_call(
        _out_proj_kernel,
        grid=(t // tm,),
        in_specs=[
            pl.BlockSpec((tm, half), lambda i: (i, 0)),
            pl.BlockSpec((tm, half), lambda i: (i, 0)),
            pl.BlockSpec((tm, d), lambda i: (i, 0)),
            pl.BlockSpec((None, half, d), lambda i: (layer, 0, 0)),
            pl.BlockSpec((None, half, d), lambda i: (layer, 1, 0)),
            vec, _mod_spec(ga, tm, t), vec, _mod_spec(sc, tm, t), _mod_spec(sh, tm, t),
        ],
        out_specs=[pl.BlockSpec((tm, d), lambda i: (i, 0)), pl.BlockSpec((tm, d), lambda i: (i, 0))],
        out_shape=[jax.ShapeDtypeStruct((t, d), F32), jax.ShapeDtypeStruct((t, d), BF16)],
        compiler_params=_params(("arbitrary",)),
        name="out_proj",
    )(ym, ya, x, w_out, w_out, g_post, ga, g_pre, sc, sh)


def _mlp_kernel(h_ref, wu_ref, wd_ref, x_ref, gpost_ref, ga_ref, xo_ref, acc_ref):
    f = pl.program_id(1)

    @pl.when(f == 0)
    def _():
        acc_ref[...] = jnp.zeros_like(acc_ref)

    a = jnp.dot(h_ref[...], wu_ref[...], preferred_element_type=F32)
    a = jnp.square(jnp.maximum(a, 0.0)).astype(BF16)
    acc_ref[...] += jnp.dot(a, wd_ref[...], preferred_element_type=F32)

    @pl.when(f == pl.num_programs(1) - 1)
    def _():
        xo_ref[...] = x_ref[...] + ga_ref[...] * (_rms(acc_ref[...]) * gpost_ref[...])


def _mlp(h, x, w_up, w_down, g_post, ga, layer):
    t, d = x.shape
    ff = w_up.shape[2]
    tm = min(512, t)
    tf = 1024
    return pl.pallas_call(
        _mlp_kernel,
        grid=(t // tm, ff // tf),
        in_specs=[
            pl.BlockSpec((tm, d), lambda i, f: (i, 0)),
            pl.BlockSpec((None, d, tf), lambda i, f: (layer, 0, f)),
            pl.BlockSpec((None, tf, d), lambda i, f: (layer, f, 0)),
            pl.BlockSpec((tm, d), lambda i, f: (i, 0)),
            pl.BlockSpec((1, d), lambda i, f: (0, 0)),
            _mod_spec(ga, tm, t),
        ],
        out_specs=pl.BlockSpec((tm, d), lambda i, f: (i, 0)),
        out_shape=jax.ShapeDtypeStruct((t, d), F32),
        scratch_shapes=[pltpu.VMEM((tm, d), F32)],
        compiler_params=_params(("arbitrary", "arbitrary")),
        name="mlp",
    )(h, w_up, w_down, x, g_post, ga)


def _prep_w_in(w_in):
    qw, vw = M_HEADS * M_DK, M_HEADS * M_DV
    o = 2 * qw + 2 * vw
    gates = w_in[..., o:o + 2 * M_HEADS]
    o += 2 * M_HEADS
    q_lat = w_in[..., o:o + Q_RANK]
    o += Q_RANK
    kv_lat = w_in[..., o:o + KV_RANK]
    o += KV_RANK
    k_r = w_in[..., o:o + A_ROPE]
    half = A_ROPE // 2
    zeros = lambda n: jnp.zeros(w_in.shape[:-1] + (n,), w_in.dtype)
    out = jnp.concatenate([
        w_in[..., :2 * qw + 2 * vw],
        q_lat,
        gates, zeros(LANES - 2 * M_HEADS),
        kv_lat,
        k_r, k_r[..., half:], k_r[..., :half],
        zeros(U_WIDTH - U_KR - LANES),
    ], axis=-1)
    assert out.shape[-1] == U_WIDTH
    return out.astype(BF16)


def _prep_w_uq(w_uq):
    depth, r, _ = w_uq.shape
    w = w_uq.reshape(depth, r, A_HEADS, A_NOPE + A_ROPE)
    half = A_ROPE // 2
    x1 = w[..., A_NOPE:A_NOPE + half]
    x2 = w[..., A_NOPE + half:]
    out = jnp.concatenate([w[..., :A_NOPE], x1, x2, x2, x1], axis=-1)
    return out.reshape(depth, r, A_HEADS * 2 * LANES).astype(BF16)


def _rope_tables(pos):
    freqs = ROPE_THETA ** (-jnp.arange(0, A_ROPE, 2, dtype=F32) / A_ROPE)
    ang = pos[:, None] * freqs[None, :]
    cos, sin = jnp.cos(ang), jnp.sin(ang)
    z = jnp.zeros((pos.shape[0], A_ROPE), F32)
    return jnp.concatenate([cos, cos, z], axis=1), jnp.concatenate([-sin, sin, z], axis=1)


def _split_mod(ada, groups, rows):
    d = ada.shape[1] // 6
    return [ada[:, i * d:(i + 1) * d].reshape(groups, rows, d) for i in range(6)]


def kernel(x_prompt, x_sample, c_prompt, c_sample, cache_kv_latent, cache_k_rope, state_C, state_n, state_m,
           page_table, w_ada, b_ada, g_pre_mix, g_post_mix, g_pre_ff, g_post_ff, w_in, b_i, b_f, g_mh,
           g_q, w_uq, g_kv, w_uk, w_uv, w_out, w_up, w_down):
    bp, lp, d = x_prompt.shape
    bs, ls, _ = x_sample.shape
    assert ls == 1
    depth = w_in.shape[0]
    tp = bp * lp

    w_in_b = _prep_w_in(w_in)
    w_uq_b = _prep_w_uq(w_uq)
    w_uk_b = w_uk.astype(BF16)
    w_uv_b = w_uv.astype(BF16)
    w_out_b = w_out.astype(BF16)
    w_up_b = w_up.astype(BF16)
    w_down_b = w_down.astype(BF16)

    pad = (-(bp + bs)) % 8
    c_all = jnp.concatenate([c_prompt, c_sample, jnp.zeros((pad, d), F32)], axis=0)
    ada = _ada(c_all, w_ada, b_ada)

    cos_p, sin_p = _rope_tables(jnp.arange(lp, dtype=F32))
    cos_s, sin_s = _rope_tables(jnp.full((bs,), PAST_LEN, F32) + jnp.arange(ls, dtype=F32))

    cache_kr_t = jnp.swapaxes(cache_k_rope, 2, 3)

    xp = x_prompt.reshape(tp, d)
    xs = x_sample.reshape(bs, d)
    outs_p = [[] for _ in range(5)]
    outs_s = [[] for _ in range(5)]
    vec = lambda a: a.reshape(1, -1)

    for l in range(depth):
        gains = dict(g_pre=vec(g_pre_mix[l]), g_post=vec(g_post_mix[l]), g_pre_ff=vec(g_pre_ff[l]),
                     g_post_ff=vec(g_post_ff[l]), g_mh=vec(g_mh[l]), g_q=vec(g_q[l]), g_kv=vec(g_kv[l]))

        sh1, sc1, ga1, sh2, sc2, ga2 = _split_mod(ada[l, :bp], bp, 1)
        u = _in_proj(xp, gains["g_pre"], sc1, sh1, w_in_b, l)
        ym, c_new, n_new, m_new = _mlstm_prompt(u, b_i[l], b_f[l], gains["g_mh"], bp, lp)
        qcat, ckv, kpe, kn, v, kpe128 = _mla_proj(u, gains["g_q"], gains["g_kv"], w_uq_b, w_uk_b, w_uv_b,
                                                  cos_p, sin_p, l, prompt=True)
        ya = _flash(qcat, kn, kpe128, v, bp, lp)
        xp, h2 = _out_proj(ym, ya, xp, w_out_b, gains["g_post"], ga1, gains["g_pre_ff"], sc2, sh2, l)
        xp = _mlp(h2, xp, w_up_b, w_down_b, gains["g_post_ff"], ga2, l)
        for acc, val in zip(outs_p, (ckv.reshape(bp, lp, KV_RANK), kpe.reshape(bp, lp, A_ROPE),
                                     c_new, n_new, m_new[:, :, 0])):
            acc.append(val)

        sh1, sc1, ga1, sh2, sc2, ga2 = _split_mod(ada[l, bp:bp + bs], 1, bs)
        u = _in_proj(xs, gains["g_pre"], sc1, sh1, w_in_b, l)
        m0 = jnp.broadcast_to(state_m[l][:, :, None], (bs, M_HEADS, LANES))
        ym, c_new, n_new, m_new = _mlstm_sample(u, b_i[l], b_f[l], gains["g_mh"], state_C, state_n, m0, l)
        qcat, ckv, kpe, qabs = _mla_proj(u, gains["g_q"], gains["g_kv"], w_uq_b, w_uk_b, None,
                                         cos_s, sin_s, l, prompt=False)
        o_lat = _decode_attention(page_table, qabs, qcat, ckv, kpe, cache_kv_latent, cache_kr_t, l)
        ya = _latent_out(o_lat, w_uv_b, l)
        xs, h2 = _out_proj(ym, ya, xs, w_out_b, gains["g_post"], ga1, gains["g_pre_ff"], sc2, sh2, l)
        xs = _mlp(h2, xs, w_up_b, w_down_b, gains["g_post_ff"], ga2, l)
        for acc, val in zip(outs_s, (ckv.reshape(bs, ls, KV_RANK), kpe.reshape(bs, ls, A_ROPE),
                                     c_new, n_new, m_new[:, :, 0])):
            acc.append(val)

    return (xp.reshape(bp, lp, d), xs.reshape(bs, ls, d),
            *[jnp.stack(a) for a in outs_p], *[jnp.stack(a) for a in outs_s])
```
